```python
import math
import jax
import jax.numpy as jnp
from jax import lax
import numpy as np

D_MODEL = 1024
BATCH = 8
SEQ = 4096
DEPTH = 2
DEC_BATCH = 32
DEC_SEQ = 1
PAST_LEN = 16384
PAGE_SIZE = 128

MOBA_WIDTH = D_MODEL // 2
MOBA_HEAD_DIM = 64
MOBA_HEADS = MOBA_WIDTH // MOBA_HEAD_DIM
MOBA_BLOCK = 256
MOBA_TOPK = 3
MOBA_Q_CHUNK = 64
S5_WIDTH = D_MODEL // 4
S5_GROUP = 16
S5_GROUPS = S5_WIDTH // S5_GROUP
S5_STATE = 64
S5_DT_MIN = 0.001
S5_DT_MAX = 0.1
GLA_VAL_WIDTH = D_MODEL // 4
GLA_DV = 64
GLA_HEADS = GLA_VAL_WIDTH // GLA_DV
GLA_DK = GLA_DV // 2
GLA_KEY_WIDTH = GLA_HEADS * GLA_DK
GLA_GATE_RANK = 16
GLA_GATE_NORM = 16.0
GLA_CHUNK = 64
MIX_WIDTH = MOBA_WIDTH + S5_WIDTH + GLA_VAL_WIDTH
IN_SPLITS = (MOBA_WIDTH, MOBA_WIDTH, MOBA_WIDTH, S5_WIDTH, GLA_KEY_WIDTH, GLA_KEY_WIDTH, GLA_VAL_WIDTH, GLA_VAL_WIDTH, GLA_GATE_RANK)
IN_COLS = sum(IN_SPLITS)
FFN_HIDDEN = -(-(8 * D_MODEL) // (3 * 256)) * 256
RMS_EPS = 1e-6

kernel_name = 'hybrid_moba_s5_gla_decoder_step'

F32 = jnp.float32


def rmsnorm(x, g):
    x32 = x.astype(F32)
    y = x32 * lax.rsqrt(jnp.mean(x32 * x32, axis=-1, keepdims=True) + RMS_EPS)
    return (y * g.astype(F32)).astype(x.dtype)


def moba_attention(q, k_all, v_all, q_pos0):
    b, t, h, dh = q.shape
    L = k_all.shape[1]
    n_full = L // MOBA_BLOCK
    n_blk = -(-L // MOBA_BLOCK)
    pad = n_blk * MOBA_BLOCK - L
    if pad:
        k_all = jnp.pad(k_all, ((0, 0), (0, pad), (0, 0), (0, 0)))
        v_all = jnp.pad(v_all, ((0, 0), (0, pad), (0, 0), (0, 0)))
    kb = k_all.reshape(b, n_blk, MOBA_BLOCK, h, dh)
    vb = v_all.reshape(b, n_blk, MOBA_BLOCK, h, dh)
    q32 = q.astype(F32)
    q_pos = q_pos0 + jnp.arange(t, dtype=jnp.int32)
    own = q_pos // MOBA_BLOCK
    own_idx = jnp.broadcast_to(own[None, :, None, None], (b, t, h, 1))
    k_sel = min(MOBA_TOPK, n_full)
    if k_sel > 0:
        means = jnp.mean(kb[:, :n_full].astype(F32), axis=2)
        gate = jnp.einsum('bthd,bnhd->bthn', q32, means)
        past_blk = jnp.arange(n_full)[None, None, None, :] < own[None, :, None, None]
        gate = jnp.where(past_blk, gate, -jnp.inf)
        _, sel = lax.top_k(gate, k_sel)
        sel = sel.astype(jnp.int32)
        idx = jnp.concatenate([sel, own_idx], axis=-1)
        valid = jnp.concatenate([sel < own[None, :, None, None], jnp.ones((b, t, h, 1), dtype=bool)], axis=-1)
    else:
        idx = own_idx
        valid = jnp.ones((b, t, h, 1), dtype=bool)
    qc = math.gcd(t, MOBA_Q_CHUNK)
    nc = t // qc

    def to_chunks(a):
        return jnp.moveaxis(a.reshape((b, nc, qc) + a.shape[2:]), 1, 0)

    gather = jax.vmap(jax.vmap(lambda blk, ix: blk[ix], in_axes=(2, 1), out_axes=1))
    scale = dh ** -0.5

    def attend(args):
        q_c, idx_c, valid_c, pos_c = args
        kg = gather(kb, idx_c).astype(F32)
        vg = gather(vb, idx_c).astype(F32)
        logits = jnp.einsum('bqhd,bqhsid->bqhsi', q_c, kg) * scale
        kpos = idx_c[..., None] * MOBA_BLOCK + jnp.arange(MOBA_BLOCK, dtype=jnp.int32)
        mask = valid_c[..., None] & (kpos <= pos_c[None, :, None, None, None])
        logits = jnp.where(mask, logits, -jnp.inf)
        shp = logits.shape
        p = jax.nn.softmax(logits.reshape(shp[:3] + (shp[3] * shp[4],)), axis=-1).reshape(shp)
        return jnp.einsum('bqhsi,bqhsid->bqhd', p, vg)

    out = lax.map(attend, (to_chunks(q32), to_chunks(idx), to_chunks(valid), q_pos.reshape(nc, qc)))
    out = jnp.moveaxis(out, 0, 1).reshape(b, t, h * dh)
    return out.astype(q.dtype)


def _complex_affine_combine(e1, e2):
    ar1, ai1, br1, bi1 = e1
    ar2, ai2, br2, bi2 = e2
    return (ar1 * ar2 - ai1 * ai2,
            ar1 * ai2 + ai1 * ar2,
            ar2 * br1 - ai2 * bi1 + br2,
            ar2 * bi1 + ai2 * br1 + bi2)


def s5_scan(u, h0_re, h0_im, a_re, a_im, log_step, b_re, b_im, c_re, c_im, d_skip):
    bsz, t, _ = u.shape
    ug = u.astype(F32).reshape(bsz, t, S5_GROUPS, S5_GROUP)
    a_re = a_re.astype(F32)
    a_im = a_im.astype(F32)
    step = jnp.exp(log_step.astype(F32))[:, None]
    mag = jnp.exp(a_re * step)
    ab_re = mag * jnp.cos(a_im * step)
    ab_im = mag * jnp.sin(a_im * step)
    den = a_re * a_re + a_im * a_im
    n_re = ab_re - 1.0
    f_re = (n_re * a_re + ab_im * a_im) / den
    f_im = (ab_im * a_re - n_re * a_im) / den
    b_re = b_re.astype(F32)
    b_im = b_im.astype(F32)
    bb_re = f_re[..., None] * b_re - f_im[..., None] * b_im
    bb_im = f_re[..., None] * b_im + f_im[..., None] * b_re
    x_re = jnp.einsum('btgc,gpc->btgp', ug, bb_re)
    x_im = jnp.einsum('btgc,gpc->btgp', ug, bb_im)
    h0_re = h0_re.astype(F32)
    h0_im = h0_im.astype(F32)
    x_re = x_re.at[:, 0].add(ab_re * h0_re - ab_im * h0_im)
    x_im = x_im.at[:, 0].add(ab_re * h0_im + ab_im * h0_re)
    aa_re = jnp.broadcast_to(ab_re, x_re.shape)
    aa_im = jnp.broadcast_to(ab_im, x_im.shape)
    _, _, h_re, h_im = lax.associative_scan(_complex_affine_combine, (aa_re, aa_im, x_re, x_im), axis=1)
    y = (jnp.einsum('btgp,gcp->btgc', h_re, c_re.astype(F32))
         - jnp.einsum('btgp,gcp->btgc', h_im, c_im.astype(F32))
         + d_skip.astype(F32).reshape(S5_GROUPS, S5_GROUP) * ug)
    return y.reshape(bsz, t, S5_WIDTH), h_re[:, -1], h_im[:, -1]


def gla_recurrent(q, k, v, g, s0):
    b, t, h, dk = q.shape
    dv = v.shape[-1]
    c = min(GLA_CHUNK, t)
    n = -(-t // c)
    pad = n * c - t
    if pad:
        pw = ((0, 0), (0, pad), (0, 0), (0, 0))
        q, k, v, g = jnp.pad(q, pw), jnp.pad(k, pw), jnp.pad(v, pw), jnp.pad(g, pw)

    def to_chunks(a):
        return jnp.moveaxis(a.reshape(b, n, c, h, a.shape[-1]), 1, 0)

    causal = jnp.tril(jnp.ones((c, c), dtype=bool))

    def step(S, inp):
        qc, kc, vc, gc = inp
        bcum = jnp.cumsum(gc, axis=1)
        o_inter = jnp.einsum('bthk,bhkv->bthv', qc * jnp.exp(bcum), S)
        diff = bcum[:, :, None] - bcum[:, None, :]
        decay = jnp.exp(jnp.where(causal[None, :, :, None, None], diff, -jnp.inf))
        att = jnp.einsum('bthk,bshk,btshk->bths', qc, kc, decay)
        o = o_inter + jnp.einsum('bths,bshv->bthv', att, vc)
        last = bcum[:, -1]
        S = jnp.exp(last)[..., None] * S + jnp.einsum('bshk,bshv->bhkv', kc * jnp.exp(last[:, None] - bcum), vc)
        return S, o

    S, o = lax.scan(step, s0.astype(F32), (to_chunks(q), to_chunks(k), to_chunks(v), to_chunks(g)))
    o = jnp.moveaxis(o, 0, 1).reshape(b, n * c, h, dv)[:, :t]
    return o, S


def trunk(x, q_pos0, kv_pool, ssm_re0, ssm_im0, gla0, w):
    (norm_mix, w_in, w_out, s5_a_re, s5_a_im, s5_log_step, s5_b_re, s5_b_im, s5_c_re, s5_c_im,
     s5_d, s5_w_glu, s5_b_glu, gla_w_gate, gla_b_gate, gla_norm, norm_ffn, w_ffn_gate, w_ffn_up,
     w_ffn_down, norm_final) = w
    b, t, _ = x.shape
    offsets = np.cumsum(IN_SPLITS)[:-1].tolist()
    k_rows, v_rows, s_re, s_im, s_gla = [], [], [], [], []
    for l in range(DEPTH):
        h = rmsnorm(x, norm_mix[l])
        z = h @ w_in[l]
        q, k, v, u, gq, gk, gv, gr, glr = jnp.split(z, offsets, axis=-1)
        q = q.reshape(b, t, MOBA_HEADS, MOBA_HEAD_DIM)
        k = k.reshape(b, t, MOBA_HEADS, MOBA_HEAD_DIM)
        v = v.reshape(b, t, MOBA_HEADS, MOBA_HEAD_DIM)
        k_rows.append(k)
        v_rows.append(v)
        if kv_pool is None:
            k_all, v_all = k, v
        else:
            cache_k, cache_v, page_table = kv_pool
            past_k = cache_k[page_table, l].reshape(b, -1, MOBA_HEADS, MOBA_HEAD_DIM).astype(k.dtype)
            past_v = cache_v[page_table, l].reshape(b, -1, MOBA_HEADS, MOBA_HEAD_DIM).astype(v.dtype)
            k_all = jnp.concatenate([past_k, k], axis=1)
            v_all = jnp.concatenate([past_v, v], axis=1)
        a_out = moba_attention(q, k_all, v_all, q_pos0)
        y_s5, h_re, h_im = s5_scan(u, ssm_re0[l], ssm_im0[l], s5_a_re[l], s5_a_im[l], s5_log_step[l],
                                   s5_b_re[l], s5_b_im[l], s5_c_re[l], s5_c_im[l], s5_d[l])
        s_re.append(h_re)
        s_im.append(h_im)
        yg = jax.nn.gelu(y_s5)
        b_out = (yg * jax.nn.sigmoid(yg @ s5_w_glu[l].astype(F32) + s5_b_glu[l].astype(F32))).astype(x.dtype)
        gq = gq.astype(F32).reshape(b, t, GLA_HEADS, GLA_DK) * (GLA_DK ** -0.5)
        gk = gk.astype(F32).reshape(b, t, GLA_HEADS, GLA_DK)
        gv = gv.astype(F32).reshape(b, t, GLA_HEADS, GLA_DV)
        logf = jax.nn.log_sigmoid((glr @ gla_w_gate[l] + gla_b_gate[l]).astype(F32)) / GLA_GATE_NORM
        logf = logf.reshape(b, t, GLA_HEADS, GLA_DK)
        o_gla, s_fin = gla_recurrent(gq, gk, gv, logf, gla0[l])
        s_gla.append(s_fin)
        o_gla = rmsnorm(o_gla, gla_norm[l]).reshape(b, t, GLA_VAL_WIDTH)
        c_out = (o_gla * jax.nn.silu(gr.astype(F32))).astype(x.dtype)
        x = x + jnp.concatenate([a_out, b_out, c_out], axis=-1) @ w_out[l]
        hf = rmsnorm(x, norm_ffn[l])
        x = x + (jax.nn.silu(hf @ w_ffn_gate[l]) * (hf @ w_ffn_up[l])) @ w_ffn_down[l]
    y = rmsnorm(x, norm_final)
    return (y, jnp.stack(k_rows, axis=1), jnp.stack(v_rows, axis=1),
            jnp.stack(s_re, axis=0), jnp.stack(s_im, axis=0), jnp.stack(s_gla, axis=0))


def setup_inputs(seed: int = 0) -> dict:
    key = jax.random.key(seed)
    ks = jax.random.split(key, 32)
    n_pages = PAST_LEN // PAGE_SIZE
    n_used = DEC_BATCH * n_pages
    n_pool = n_used + n_used // 4
    nrm = jax.random.normal
    kv_shape = (n_pool, DEPTH, PAGE_SIZE, MOBA_HEADS, MOBA_HEAD_DIM)
    a_im_base = jnp.pi * jnp.arange(S5_STATE, dtype=F32)
    return {
        'x_prompt': nrm(ks[0], (BATCH, SEQ, D_MODEL), F32),
        'x_sample': nrm(ks[1], (DEC_BATCH, DEC_SEQ, D_MODEL), F32),
        'cache_k': nrm(ks[2], kv_shape, F32),
        'cache_v': nrm(ks[3], kv_shape, F32),
        'page_table': jax.random.permutation(ks[4], n_pool)[:n_used].reshape(DEC_BATCH, n_pages).astype(jnp.int32),
        'state_ssm_re': 0.1 * nrm(ks[5], (DEPTH, DEC_BATCH, S5_GROUPS, S5_STATE), F32),
        'state_ssm_im': 0.1 * nrm(ks[6], (DEPTH, DEC_BATCH, S5_GROUPS, S5_STATE), F32),
        'state_gla': 0.5 * nrm(ks[7], (DEPTH, DEC_BATCH, GLA_HEADS, GLA_DK, GLA_DV), F32),
        'norm_mix': 1.0 + 0.02 * nrm(ks[8], (DEPTH, D_MODEL), F32),
        'w_in': nrm(ks[9], (DEPTH, D_MODEL, IN_COLS), F32) * D_MODEL ** -0.5,
        'w_out': nrm(ks[10], (DEPTH, MIX_WIDTH, D_MODEL), F32) * MIX_WIDTH ** -0.5,
        's5_a_re': -0.5 + 0.01 * nrm(ks[11], (DEPTH, S5_GROUPS, S5_STATE), F32),
        's5_a_im': a_im_base + 0.01 * nrm(ks[12], (DEPTH, S5_GROUPS, S5_STATE), F32),
        's5_log_step': jax.random.uniform(ks[13], (DEPTH, S5_GROUPS), F32, math.log(S5_DT_MIN), math.log(S5_DT_MAX)),
        's5_b_re': nrm(ks[14], (DEPTH, S5_GROUPS, S5_STATE, S5_GROUP), F32) * (2.0 * S5_GROUP) ** -0.5,
        's5_b_im': nrm(ks[15], (DEPTH, S5_GROUPS, S5_STATE, S5_GROUP), F32) * (2.0 * S5_GROUP) ** -0.5,
        's5_c_re': nrm(ks[16], (DEPTH, S5_GROUPS, S5_GROUP, S5_STATE), F32) * (2.0 * S5_STATE) ** -0.5,
        's5_c_im': nrm(ks[17], (DEPTH, S5_GROUPS, S5_GROUP, S5_STATE), F32) * (2.0 * S5_STATE) ** -0.5,
        's5_d': nrm(ks[18], (DEPTH, S5_WIDTH), F32),
        's5_w_glu': nrm(ks[19], (DEPTH, S5_WIDTH, S5_WIDTH), F32) * S5_WIDTH ** -0.5,
        's5_b_glu': 0.01 * nrm(ks[20], (DEPTH, S5_WIDTH), F32),
        'gla_w_gate': nrm(ks[21], (DEPTH, GLA_GATE_RANK, GLA_KEY_WIDTH), F32) * GLA_GATE_RANK ** -0.5,
        'gla_b_gate': 0.01 * nrm(ks[22], (DEPTH, GLA_KEY_WIDTH), F32),
        'gla_norm': 1.0 + 0.02 * nrm(ks[23], (DEPTH, GLA_DV), F32),
        'norm_ffn': 1.0 + 0.02 * nrm(ks[24], (DEPTH, D_MODEL), F32),
        'w_ffn_gate': nrm(ks[25], (DEPTH, D_MODEL, FFN_HIDDEN), F32) * D_MODEL ** -0.5,
        'w_ffn_up': nrm(ks[26], (DEPTH, D_MODEL, FFN_HIDDEN), F32) * D_MODEL ** -0.5,
        'w_ffn_down': nrm(ks[27], (DEPTH, FFN_HIDDEN, D_MODEL), F32) * FFN_HIDDEN ** -0.5,
        'norm_final': 1.0 + 0.02 * nrm(ks[28], (D_MODEL,), F32),
    }


def reference(x_prompt, x_sample, cache_k, cache_v, page_table, state_ssm_re, state_ssm_im, state_gla,
              norm_mix, w_in, w_out, s5_a_re, s5_a_im, s5_log_step, s5_b_re, s5_b_im, s5_c_re, s5_c_im,
              s5_d, s5_w_glu, s5_b_glu, gla_w_gate, gla_b_gate, gla_norm, norm_ffn, w_ffn_gate, w_ffn_up,
              w_ffn_down, norm_final):
    w = (norm_mix, w_in, w_out, s5_a_re, s5_a_im, s5_log_step, s5_b_re, s5_b_im, s5_c_re, s5_c_im,
         s5_d, s5_w_glu, s5_b_glu, gla_w_gate, gla_b_gate, gla_norm, norm_ffn, w_ffn_gate, w_ffn_up,
         w_ffn_down, norm_final)
    bp = x_prompt.shape[0]
    zero_ssm = jnp.zeros((DEPTH, bp, S5_GROUPS, S5_STATE), F32)
    zero_gla = jnp.zeros((DEPTH, bp, GLA_HEADS, GLA_DK, GLA_DV), F32)
    y_prompt, k_new_prompt, v_new_prompt, ssm_re_prompt, ssm_im_prompt, gla_prompt = trunk(
        x_prompt, 0, None, zero_ssm, zero_ssm, zero_gla, w)
    y_sample, k_new_sample, v_new_sample, ssm_re_sample, ssm_im_sample, gla_sample = trunk(
        x_sample, PAST_LEN, (cache_k, cache_v, page_table), state_ssm_re, state_ssm_im, state_gla, w)
    return (y_prompt, y_sample, k_new_prompt, v_new_prompt, k_new_sample, v_new_sample,
            ssm_re_prompt, ssm_im_prompt, ssm_re_sample, ssm_im_sample, gla_prompt, gla_sample)
```

```python
import functools
import math

import jax
import jax.numpy as jnp
from jax import lax
from jax.experimental import pallas as pl
from jax.experimental.pallas import tpu as pltpu

F32 = jnp.float32
BF16 = jnp.bfloat16

D_MODEL = 1024
PAGE_SIZE = 128
MOBA_HEAD_DIM = 64
MOBA_HEADS = 8
MOBA_WIDTH = MOBA_HEADS * MOBA_HEAD_DIM
MOBA_BLOCK = 256
MOBA_TOPK = 3
S5_WIDTH = 256
S5_GROUP = 16
S5_GROUPS = 16
S5_STATE = 64
S5_CH = S5_GROUPS * S5_STATE
GLA_HEADS = 4
GLA_DK = 32
GLA_DV = 64
GLA_KEY_WIDTH = GLA_HEADS * GLA_DK
GLA_VAL_WIDTH = GLA_HEADS * GLA_DV
GLA_GATE_RANK = 16
GLA_GATE_NORM = 16.0
GLA_CHUNK = 64
FFN_HIDDEN = 2816
RMS_EPS = 1e-6
NEG_INF = float("-inf")

_OFF_Q = 0
_OFF_K = _OFF_Q + MOBA_WIDTH
_OFF_V = _OFF_K + MOBA_WIDTH
_OFF_U = _OFF_V + MOBA_WIDTH
_OFF_GQ = _OFF_U + S5_WIDTH
_OFF_GK = _OFF_GQ + GLA_KEY_WIDTH
_OFF_GV = _OFF_GK + GLA_KEY_WIDTH
_OFF_GR = _OFF_GV + GLA_VAL_WIDTH
_OFF_GLR = _OFF_GR + GLA_VAL_WIDTH
IN_COLS = _OFF_GLR + GLA_GATE_RANK
LANE = 128
IN_COLS_PAD = _OFF_GLR + LANE

VMEM_LIMIT = 56 * 1024 * 1024


def _cparams(sem):
    return pltpu.CompilerParams(dimension_semantics=sem, vmem_limit_bytes=VMEM_LIMIT)


def _nt(a, b):
    return lax.dot_general(a, b, (((1,), (1,)), ((), ())), preferred_element_type=F32)


def _tn(a, b):
    return lax.dot_general(a, b, (((0,), (0,)), ((), ())), preferred_element_type=F32)


def _mm(a, b):
    return jnp.dot(a, b, preferred_element_type=F32)


def _sigmoid(x):
    return 1.0 / (1.0 + jnp.exp(-x))


def _silu(x):
    return x * _sigmoid(x)


def _gelu_tanh(x):
    c = math.sqrt(2.0 / math.pi)
    return 0.5 * x * (1.0 + jnp.tanh(c * (x + 0.044715 * (x * x * x))))


def _log_sigmoid(x):
    return jnp.minimum(x, 0.0) - jnp.log(1.0 + jnp.exp(-jnp.abs(x)))


def _rms(x, g):
    return x * lax.rsqrt(jnp.mean(x * x, axis=-1, keepdims=True) + RMS_EPS) * g


def _in_proj_kernel(x_ref, g_ref, w_ref, wg_ref, bg_ref,
                    k_ref, v_ref, qh_ref, kh_ref, vh_ref, u_ref,
                    gq_ref, gk_ref, gv_ref, gr_ref, lf_ref):
    h = _rms(x_ref[...], g_ref[...]).astype(BF16)

    def proj(lo, width):
        return _mm(h, w_ref[:, lo:lo + width])

    q = proj(_OFF_Q, MOBA_WIDTH)
    k = proj(_OFF_K, MOBA_WIDTH)
    v = proj(_OFF_V, MOBA_WIDTH)
    k_ref[...] = k
    v_ref[...] = v
    for hd in range(MOBA_HEADS):
        sl = slice(hd * MOBA_HEAD_DIM, (hd + 1) * MOBA_HEAD_DIM)
        qh_ref[hd] = q[:, sl].astype(BF16)
        kh_ref[hd] = k[:, sl].astype(BF16)
        vh_ref[hd] = v[:, sl].astype(BF16)
    u_ref[...] = proj(_OFF_U, S5_WIDTH)
    gq_ref[...] = proj(_OFF_GQ, GLA_KEY_WIDTH) * (GLA_DK ** -0.5)
    gk_ref[...] = proj(_OFF_GK, GLA_KEY_WIDTH)
    gv_ref[...] = proj(_OFF_GV, GLA_VAL_WIDTH)
    gr_ref[...] = proj(_OFF_GR, GLA_VAL_WIDTH)
    glr = proj(_OFF_GLR, LANE)
    gate = _mm(glr.astype(BF16), wg_ref[...]) + bg_ref[...]
    lf_ref[...] = _log_sigmoid(gate) * (1.0 / GLA_GATE_NORM)


def _in_proj(x2d, nseq, seqlen, norm_g, w_pad, wg_pad, bg):
    n = nseq * seqlen
    tm = min(512, seqlen)
    nt = seqlen // tm
    tok = lambda width: pl.BlockSpec((tm, width), lambda b, t: (b * nt + t, 0))
    head = pl.BlockSpec((None, MOBA_HEADS, tm, MOBA_HEAD_DIM), lambda b, t: (b, 0, t, 0))
    const = lambda shape: pl.BlockSpec(shape, lambda b, t: (0,) * len(shape))
    head_shape = jax.ShapeDtypeStruct((nseq, MOBA_HEADS, seqlen, MOBA_HEAD_DIM), BF16)
    tokf = lambda width: jax.ShapeDtypeStruct((n, width), F32)
    return pl.pallas_call(
        _in_proj_kernel,
        grid=(nseq, nt),
        in_specs=[tok(D_MODEL), const((1, D_MODEL)), const((D_MODEL, IN_COLS_PAD)),
                  const((LANE, GLA_KEY_WIDTH)), const((1, GLA_KEY_WIDTH))],
        out_specs=[tok(MOBA_WIDTH), tok(MOBA_WIDTH), head, head, head,
                   pl.BlockSpec((tm, S5_WIDTH), lambda b, t: (t, b)),
                   tok(GLA_KEY_WIDTH), tok(GLA_KEY_WIDTH), tok(GLA_VAL_WIDTH), tok(GLA_VAL_WIDTH),
                   tok(GLA_KEY_WIDTH)],
        out_shape=[tokf(MOBA_WIDTH), tokf(MOBA_WIDTH), head_shape, head_shape, head_shape,
                   jax.ShapeDtypeStruct((seqlen, nseq * S5_WIDTH), F32),
                   tokf(GLA_KEY_WIDTH), tokf(GLA_KEY_WIDTH), tokf(GLA_VAL_WIDTH), tokf(GLA_VAL_WIDTH),
                   tokf(GLA_KEY_WIDTH)],
        compiler_params=_cparams(("parallel", "parallel")),
        name="in_proj",
    )(x2d, norm_g, w_pad, wg_pad, bg)


def _moba_prompt_kernel(q_ref, k_ref, v_ref, o_ref, means_ref, sel_ref, *, nblk, nblk_pad):
    i = pl.program_id(2)
    bs = MOBA_BLOCK
    scale = MOBA_HEAD_DIM ** -0.5

    @pl.when(i == 0)
    def _():
        means_ref[...] = jnp.zeros_like(means_ref)
        for hh in range(2):
            for j in range(nblk):
                kj = k_ref[hh, j * bs:(j + 1) * bs, :].astype(F32)
                means_ref[hh, j:j + 1, :] = jnp.mean(kj, axis=0, keepdims=True)

    blk_iota = lax.broadcasted_iota(jnp.int32, (nblk_pad, bs), 0)
    key_iota = lax.broadcasted_iota(jnp.int32, (bs, bs), 0)
    qry_iota = lax.broadcasted_iota(jnp.int32, (bs, bs), 1)
    outs = []
    for hh in range(2):
        qb = q_ref[hh]
        gate = _nt(means_ref[hh].astype(BF16), qb)
        gate = jnp.where(blk_iota < i, gate, NEG_INF)
        for j in range(nblk):
            gj = gate[j:j + 1, :]
            ahead = (gate > gj) | ((gate == gj) & (blk_iota < j))
            cnt = jnp.sum(ahead.astype(F32), axis=0, keepdims=True)
            is_past = jnp.where(j < i, 1.0, 0.0)
            sel_ref[hh, j:j + 1, :] = jnp.where(cnt < MOBA_TOPK, is_past, 0.0)

        r0 = pl.multiple_of(i * bs, bs)
        s = _nt(k_ref[hh, pl.ds(r0, bs), :], qb) * scale
        s = jnp.where(key_iota <= qry_iota, s, NEG_INF)
        m = jnp.max(s, axis=0, keepdims=True)
        p = jnp.exp(s - m)
        l = jnp.sum(p, axis=0, keepdims=True)
        acc = _tn(v_ref[hh, pl.ds(r0, bs), :], p.astype(BF16))

        def body(j, carry, hh=hh, qb=qb):
            m, l, acc = carry
            c0 = pl.multiple_of(j * bs, bs)
            s = _nt(k_ref[hh, pl.ds(c0, bs), :], qb) * scale
            s = jnp.where(sel_ref[hh, pl.ds(j, 1), :] > 0.0, s, NEG_INF)
            m_new = jnp.maximum(m, jnp.max(s, axis=0, keepdims=True))
            alpha = jnp.exp(m - m_new)
            p = jnp.exp(s - m_new)
            l = alpha * l + jnp.sum(p, axis=0, keepdims=True)
            acc = alpha * acc + _tn(v_ref[hh, pl.ds(c0, bs), :], p.astype(BF16))
            return m_new, l, acc

        m, l, acc = lax.fori_loop(0, i, body, (m, l, acc))
        outs.append(acc / l)
    o_ref[...] = jnp.concatenate(outs, axis=0).T


def _moba_prompt(qh, kh, vh):
    nseq, _, seqlen, dh = qh.shape
    bs = MOBA_BLOCK
    nblk = seqlen // bs
    nblk_pad = -(-nblk // 8) * 8
    kern = functools.partial(_moba_prompt_kernel, nblk=nblk, nblk_pad=nblk_pad)
    return pl.pallas_call(
        kern,
        grid=(nseq, MOBA_HEADS // 2, nblk),
        in_specs=[pl.BlockSpec((None, 2, bs, dh), lambda b, h, i: (b, h, i, 0)),
                  pl.BlockSpec((None, 2, seqlen, dh), lambda b, h, i: (b, h, 0, 0)),
                  pl.BlockSpec((None, 2, seqlen, dh), lambda b, h, i: (b, h, 0, 0))],
        out_specs=pl.BlockSpec((bs, 2 * dh), lambda b, h, i: (b * nblk + i, h)),
        out_shape=jax.ShapeDtypeStruct((nseq * seqlen, MOBA_WIDTH), F32),
        scratch_shapes=[pltpu.VMEM((2, nblk_pad, dh), F32), pltpu.VMEM((2, nblk_pad, bs), F32)],
        compiler_params=_cparams(("parallel", "parallel", "arbitrary")),
        name="moba_prompt",
    )(qh, kh, vh)


def _moba_decode_kernel(pt_ref, q_ref, kn_ref, vn_ref, *rest, ppg, nblk):
    del pt_ref
    k_refs = rest[:ppg]
    v_refs = rest[ppg:2 * ppg]
    o_ref = rest[2 * ppg]
    g_ref, m_ref, l_ref, ob_ref = rest[2 * ppg + 1:]
    s_id = pl.program_id(1)
    nsteps = pl.num_programs(1)
    scale = MOBA_HEAD_DIM ** -0.5
    width = MOBA_WIDTH

    head_of_lane = lax.broadcasted_iota(jnp.int32, (MOBA_HEADS, width), 1) // MOBA_HEAD_DIM
    head_row = lax.broadcasted_iota(jnp.int32, (MOBA_HEADS, width), 0)
    own = head_of_lane == head_row
    q_row = q_ref[...]
    q_bd = jnp.where(own, q_row, 0.0)
    q_bd16 = q_bd.astype(BF16)
    lane_blk = lax.broadcasted_iota(jnp.int32, (MOBA_HEADS, LANE), 1)

    @pl.when(s_id == 0)
    def _():
        g_ref[...] = jnp.full_like(g_ref, NEG_INF)
        m_ref[...] = jnp.zeros_like(m_ref)
        l_ref[...] = jnp.zeros_like(l_ref)

    for bi in range(ppg // 2):
        kb = jnp.concatenate([k_refs[2 * bi][...], k_refs[2 * bi + 1][...]], axis=0)
        vb = jnp.concatenate([v_refs[2 * bi][...], v_refs[2 * bi + 1][...]], axis=0)
        n = s_id * (ppg // 2) + bi
        mean = jnp.mean(kb, axis=0, keepdims=True)
        gate = jnp.sum(q_bd * mean, axis=1, keepdims=True)
        s = _nt(q_bd16, kb.astype(BF16)) * scale
        mb = jnp.max(s, axis=1, keepdims=True)
        p = jnp.exp(s - mb)
        lb = jnp.sum(p, axis=1, keepdims=True)
        ob_ref[n] = _mm(p.astype(BF16), vb.astype(BF16))
        hit = lane_blk == n
        g_ref[...] = jnp.where(hit, gate, g_ref[...])
        m_ref[...] = jnp.where(hit, mb, m_ref[...])
        l_ref[...] = jnp.where(hit, lb, l_ref[...])

    @pl.when(s_id == nsteps - 1)
    def _():
        g = g_ref[...]
        sel = jnp.zeros(g.shape, jnp.bool_)
        lane_f = lane_blk.astype(F32)
        for _ in range(MOBA_TOPK):
            mx = jnp.max(g, axis=1, keepdims=True)
            idx = jnp.min(jnp.where(g == mx, lane_f, float(LANE)), axis=1, keepdims=True)
            pick = lane_f == idx
            sel = sel | pick
            g = jnp.where(pick, NEG_INF, g)
        s_own = jnp.sum(q_bd * kn_ref[...], axis=1, keepdims=True) * scale
        m_all = m_ref[...]
        mx = jnp.maximum(jnp.max(jnp.where(sel, m_all, NEG_INF), axis=1, keepdims=True), s_own)
        w = jnp.where(sel, jnp.exp(m_all - mx), 0.0)
        w_own = jnp.exp(s_own - mx)
        den = jnp.sum(w * l_ref[...], axis=1, keepdims=True) + w_own

        def add_block(n, num):
            wn = jnp.sum(jnp.where(lane_blk == n, w, 0.0), axis=1, keepdims=True)
            return num + wn * ob_ref[n]

        num = lax.fori_loop(0, nblk, add_block, w_own * jnp.where(own, vn_ref[...], 0.0))
        o_ref[...] = jnp.sum(jnp.where(own, num / den, 0.0), axis=0, keepdims=True)


def _moba_decode(page_table, q, k_new, v_new, cache_k4, cache_v4, layer):
    nseq, n_pages = page_table.shape
    nblk = n_pages * PAGE_SIZE // MOBA_BLOCK
    assert n_pages * PAGE_SIZE == nblk * MOBA_BLOCK and MOBA_TOPK <= nblk <= LANE
    ppg = 8 if n_pages % 8 == 0 else 2
    nsteps = n_pages // ppg
    width = MOBA_WIDTH
    row = pl.BlockSpec((None, 1, width), lambda b, s, pt: (b, 0, 0))

    def page_spec(p):
        return pl.BlockSpec((None, None, PAGE_SIZE, width),
                            lambda b, s, pt, p=p: (pt[b, s * ppg + p], layer, 0, 0))

    grid_spec = pltpu.PrefetchScalarGridSpec(
        num_scalar_prefetch=1,
        grid=(nseq, nsteps),
        in_specs=[row, row, row] + [page_spec(p) for p in range(ppg)] * 2,
        out_specs=row,
        scratch_shapes=[pltpu.VMEM((MOBA_HEADS, LANE), F32)] * 3
        + [pltpu.VMEM((nblk, MOBA_HEADS, width), F32)],
    )
    kern = functools.partial(_moba_decode_kernel, ppg=ppg, nblk=nblk)
    return pl.pallas_call(
        kern,
        grid_spec=grid_spec,
        out_shape=jax.ShapeDtypeStruct((nseq, 1, width), F32),
        compiler_params=_cparams(("parallel", "arbitrary")),
        name="moba_decode",
    )(page_table, q, k_new, v_new, *([cache_k4] * ppg), *([cache_v4] * ppg))


def _s5_disc_kernel(are_ref, aim_ref, ls_ref, bre_ref, bim_ref, abr_ref, abi_ref, bbr_ref, bbi_ref):
    a_re = are_ref[...]
    a_im = aim_ref[...]
    step = jnp.exp(ls_ref[...])
    mag = jnp.exp(a_re * step)
    ab_re = mag * jnp.cos(a_im * step)
    ab_im = mag * jnp.sin(a_im * step)
    den = a_re * a_re + a_im * a_im
    n_re = ab_re - 1.0
    f_re = (n_re * a_re + ab_im * a_im) / den
    f_im = (ab_im * a_re - n_re * a_im) / den
    b_re = bre_ref[...]
    b_im = bim_ref[...]
    abr_ref[...] = ab_re
    abi_ref[...] = ab_im
    bbr_ref[...] = f_re * b_re - f_im * b_im
    bbi_ref[...] = f_re * b_im + f_im * b_re


def _s5_discretise(a_re, a_im, log_step, b_re, b_im):
    depth = a_re.shape[0]
    rows = depth * S5_GROUPS * S5_GROUP
    rep = lambda a: jnp.broadcast_to(a[:, :, None, :], (depth, S5_GROUPS, S5_GROUP, S5_STATE)).reshape(rows, S5_STATE)
    ls = jnp.broadcast_to(log_step[:, :, None, None], (depth, S5_GROUPS, S5_GROUP, S5_STATE)).reshape(rows, S5_STATE)
    bt = lambda b: jnp.swapaxes(b, 2, 3).reshape(rows, S5_STATE)
    shp = jax.ShapeDtypeStruct((rows, S5_STATE), F32)
    abr, abi, bbr, bbi = pl.pallas_call(
        _s5_disc_kernel, out_shape=[shp] * 4, name="s5_disc",
    )(rep(a_re), rep(a_im), ls, bt(b_re), bt(b_im))
    r4 = lambda a: a.reshape(depth, S5_GROUPS, S5_GROUP, S5_STATE)
    ab = jnp.concatenate([r4(abr)[:, :, 0].reshape(depth, 1, S5_CH),
                          r4(abi)[:, :, 0].reshape(depth, 1, S5_CH)], axis=1)
    eye = jnp.eye(S5_GROUPS, dtype=F32)
    bd = lambda a: (r4(a)[:, :, :, None, :] * eye[None, :, None, :, None]).reshape(depth, S5_WIDTH, S5_CH)
    bb = jnp.concatenate([bd(bbr), bd(bbi)], axis=2)
    return ab, bb.astype(BF16)


def _s5_out_map(c_re, c_im):
    depth = c_re.shape[0]
    eye = jnp.eye(S5_GROUPS, dtype=F32)
    bd = lambda c: (jnp.swapaxes(c, 2, 3)[:, :, :, None, :] * eye[None, :, None, :, None]).reshape(depth, S5_CH, S5_WIDTH)
    return jnp.concatenate([bd(c_re), -bd(c_im)], axis=1).astype(BF16)


S5_SCAN_LANES = 512


def _s5_kernel(u_ref, bb_ref, ab_ref, h0_ref, cc_ref, d_ref, wglu_ref, bglu_ref,
               o_ref, hfin_ref, xs_ref, hc_ref, *, nb, tt):
    @pl.when(pl.program_id(0) == 0)
    def _():
        hc_ref[...] = h0_ref[...]

    u = u_ref[...]
    xs_ref[...] = _mm(u.astype(BF16), bb_ref[...])

    for c0 in range(0, S5_CH, S5_SCAN_LANES):
        re_sl = slice(c0, c0 + S5_SCAN_LANES)
        im_sl = slice(S5_CH + c0, S5_CH + c0 + S5_SCAN_LANES)
        ar = jnp.broadcast_to(ab_ref[0:1, re_sl], (nb, S5_SCAN_LANES))
        ai = jnp.broadcast_to(ab_ref[1:2, re_sl], (nb, S5_SCAN_LANES))

        def step(t, carry, re_sl=re_sl, im_sl=im_sl, ar=ar, ai=ai):
            hr, hi = carry
            rows = pl.ds(pl.multiple_of(t * nb, nb), nb)
            nhr = ar * hr - ai * hi + xs_ref[rows, re_sl]
            nhi = ar * hi + ai * hr + xs_ref[rows, im_sl]
            xs_ref[rows, re_sl] = nhr
            xs_ref[rows, im_sl] = nhi
            return nhr, nhi

        hr, hi = lax.fori_loop(0, tt, step, (hc_ref[:, re_sl], hc_ref[:, im_sl]), unroll=min(4, tt))
        hc_ref[:, re_sl] = hr
        hc_ref[:, im_sl] = hi

    y = _mm(xs_ref[...].astype(BF16), cc_ref[...]) + d_ref[...] * u
    yg = _gelu_tanh(y)
    gate = _sigmoid(_mm(yg.astype(BF16), wglu_ref[...]) + bglu_ref[...])
    o_ref[...] = yg * gate
    hfin_ref[...] = hc_ref[...]


def _s5(u_tm, nb, bb, ab, h0, cc, d, wglu, bglu):
    rows = u_tm.shape[0]
    t_total = rows // nb
    tt = min(64, t_total)
    const = lambda shape: pl.BlockSpec(shape, lambda t: (0,) * len(shape))
    kern = functools.partial(_s5_kernel, nb=nb, tt=tt)
    return pl.pallas_call(
        kern,
        grid=(t_total // tt,),
        in_specs=[pl.BlockSpec((tt * nb, S5_WIDTH), lambda t: (t, 0)),
                  const((S5_WIDTH, 2 * S5_CH)), const((2, S5_CH)), const((nb, 2 * S5_CH)),
                  const((2 * S5_CH, S5_WIDTH)), const((1, S5_WIDTH)),
                  const((S5_WIDTH, S5_WIDTH)), const((1, S5_WIDTH))],
        out_specs=[pl.BlockSpec((tt * nb, S5_WIDTH), lambda t: (t, 0)), const((nb, 2 * S5_CH))],
        out_shape=[jax.ShapeDtypeStruct((rows, S5_WIDTH), F32), jax.ShapeDtypeStruct((nb, 2 * S5_CH), F32)],
        scratch_shapes=[pltpu.VMEM((tt * nb, 2 * S5_CH), F32), pltpu.VMEM((nb, 2 * S5_CH), F32)],
        compiler_params=_cparams(("arbitrary",)),
        name="s5_scan",
    )(u_tm, bb, ab, h0, cc, d, wglu, bglu)


def _split3(x):
    hi = x.astype(BF16)
    r = x - hi.astype(F32)
    mid = r.astype(BF16)
    lo = (r - mid.astype(F32)).astype(BF16)
    return hi, mid, lo


def _gla_prompt_kernel(gq_ref, gk_ref, gv_ref, gr_ref, lf_ref, gn_ref, o_ref, sfin_ref, st_ref, *, nchunk):
    c = GLA_CHUNK

    @pl.when(pl.program_id(1) == 0)
    def _():
        st_ref[...] = jnp.zeros_like(st_ref)

    row = lax.broadcasted_iota(jnp.int32, (c, c), 0)
    col = lax.broadcasted_iota(jnp.int32, (c, c), 1)
    causal = col <= row
    tri = causal.astype(BF16)
    gn = gn_ref[...]

    for ci in range(nchunk):
        rows = slice(ci * c, (ci + 1) * c)
        g_hi, g_mid, g_lo = _split3(lf_ref[rows, :])
        bcum = _mm(tri, g_hi) + _mm(tri, g_mid) + _mm(tri, g_lo)
        blast = bcum[c - 1:c, :]
        qd = (gq_ref[rows, :] * jnp.exp(bcum)).astype(BF16)
        kd = (gk_ref[rows, :] * jnp.exp(-bcum)).astype(BF16)
        kt = (gk_ref[rows, :] * jnp.exp(blast - bcum)).astype(BF16)
        st = st_ref[...]
        st_ref[...] = st * jnp.exp(blast)
        st16 = st.astype(BF16)
        for hd in range(GLA_HEADS):
            ks = slice(hd * GLA_DK, (hd + 1) * GLA_DK)
            vs = slice(hd * GLA_DV, (hd + 1) * GLA_DV)
            vh = gv_ref[rows, vs].astype(BF16)
            att = jnp.where(causal, _nt(qd[:, ks], kd[:, ks]), 0.0)
            o = _nt(qd[:, ks], st16[:, ks]) + _mm(att.astype(BF16), vh)
            st_ref[:, ks] += _tn(vh, kt[:, ks])
            o_ref[rows, vs] = _rms(o, gn) * _silu(gr_ref[rows, vs])

    sfin_ref[...] = st_ref[...]


def _gla_prompt(gq, gk, gv, gr, lf, gnorm, nseq, seqlen):
    assert seqlen % GLA_CHUNK == 0
    tt = min(256, seqlen)
    nt = seqlen // tt
    tok = lambda width: pl.BlockSpec((tt, width), lambda b, t: (b * nt + t, 0))
    kern = functools.partial(_gla_prompt_kernel, nchunk=tt // GLA_CHUNK)
    return pl.pallas_call(
        kern,
        grid=(nseq, nt),
        in_specs=[tok(GLA_KEY_WIDTH), tok(GLA_KEY_WIDTH), tok(GLA_VAL_WIDTH), tok(GLA_VAL_WIDTH),
                  tok(GLA_KEY_WIDTH), pl.BlockSpec((1, GLA_DV), lambda b, t: (0, 0))],
        out_specs=[tok(GLA_VAL_WIDTH),
                   pl.BlockSpec((None, GLA_DV, GLA_KEY_WIDTH), lambda b, t: (b, 0, 0))],
        out_shape=[jax.ShapeDtypeStruct((nseq * seqlen, GLA_VAL_WIDTH), F32),
                   jax.ShapeDtypeStruct((nseq, GLA_DV, GLA_KEY_WIDTH), F32)],
        scratch_shapes=[pltpu.VMEM((GLA_DV, GLA_KEY_WIDTH), F32)],
        compiler_params=_cparams(("parallel", "arbitrary")),
        name="gla_prompt",
    )(gq, gk, gv, gr, lf, gnorm)


def _gla_decode_kernel(q_ref, k_ref, g_ref, v_ref, r_ref, s0_ref, gn_ref, o_ref, s_ref):
    s = jnp.exp(g_ref[...]) * s0_ref[...] + k_ref[...] * v_ref[...]
    s_ref[...] = s
    o = jnp.sum(q_ref[...] * s, axis=1, keepdims=True)
    o_ref[...] = _rms(o, gn_ref[...]) * _silu(r_ref[...])


def _gla_decode(gq, gk, lf, gv, gr, s0, gnorm):
    n = gq.shape[0]
    col = lambda a: a.reshape(n, GLA_HEADS, GLA_DK, 1)
    rowv = lambda a: a.reshape(n, GLA_HEADS, 1, GLA_DV)
    cspec = pl.BlockSpec((None, GLA_HEADS, GLA_DK, 1), lambda b: (b, 0, 0, 0))
    rspec = pl.BlockSpec((None, GLA_HEADS, 1, GLA_DV), lambda b: (b, 0, 0, 0))
    sspec = pl.BlockSpec((None, GLA_HEADS, GLA_DK, GLA_DV), lambda b: (b, 0, 0, 0))
    o, s = pl.pallas_call(
        _gla_decode_kernel,
        grid=(n,),
        in_specs=[cspec, cspec, cspec, rspec, rspec, sspec,
                  pl.BlockSpec((1, 1, GLA_DV), lambda b: (0, 0, 0))],
        out_specs=[rspec, sspec],
        out_shape=[jax.ShapeDtypeStruct((n, GLA_HEADS, 1, GLA_DV), F32),
                   jax.ShapeDtypeStruct((n, GLA_HEADS, GLA_DK, GLA_DV), F32)],
        compiler_params=_cparams(("parallel",)),
        name="gla_decode",
    )(col(gq), col(gk), col(lf), rowv(gv), rowv(gr), s0, gnorm.reshape(1, 1, GLA_DV))
    return o.reshape(n, GLA_VAL_WIDTH), s


def _out_ffn_kernel(x_ref, a_ref, b_ref, c_ref, wo_ref, nf_ref, wg_ref, wu_ref, wd_ref, nfin_ref,
                    o_ref, hf_ref, acc_ref, *, final_norm):
    j = pl.program_id(2)

    @pl.when(j == 0)
    def _():
        mix = jnp.concatenate([a_ref[...], b_ref[...], c_ref[...]], axis=1).astype(BF16)
        x1 = x_ref[...] + _mm(mix, wo_ref[...])
        acc_ref[...] = x1
        hf_ref[...] = _rms(x1, nf_ref[...]).astype(BF16)

    hf = hf_ref[...]
    act = _silu(_mm(hf, wg_ref[...])) * _mm(hf, wu_ref[...])
    acc_ref[...] += _mm(act.astype(BF16), wd_ref[...])

    @pl.when(j == pl.num_programs(2) - 1)
    def _():
        y = acc_ref[...]
        o_ref[...] = _rms(y, nfin_ref[...]) if final_norm else y


def _out_ffn(x2d, a, b_tm, c, nseq, seqlen, w_out, norm_ffn, w_gate, w_up, w_down, norm_final, final_norm):
    tm = min(512, seqlen)
    nt = seqlen // tm
    th = 256
    nh = FFN_HIDDEN // th
    tok = lambda width: pl.BlockSpec((tm, width), lambda b, t, j: (b * nt + t, 0))
    const = lambda shape: pl.BlockSpec(shape, lambda b, t, j: (0,) * len(shape))
    kern = functools.partial(_out_ffn_kernel, final_norm=final_norm)
    return pl.pallas_call(
        kern,
        grid=(nseq, nt, nh),
        in_specs=[tok(D_MODEL), tok(MOBA_WIDTH),
                  pl.BlockSpec((tm, S5_WIDTH), lambda b, t, j: (t, b)),
                  tok(GLA_VAL_WIDTH), const((D_MODEL, D_MODEL)), const((1, D_MODEL)),
                  pl.BlockSpec((D_MODEL, th), lambda b, t, j: (0, j)),
                  pl.BlockSpec((D_MODEL, th), lambda b, t, j: (0, j)),
                  pl.BlockSpec((th, D_MODEL), lambda b, t, j: (j, 0)),
                  const((1, D_MODEL))],
        out_specs=tok(D_MODEL),
        out_shape=jax.ShapeDtypeStruct((nseq * seqlen, D_MODEL), F32),
        scratch_shapes=[pltpu.VMEM((tm, D_MODEL), BF16), pltpu.VMEM((tm, D_MODEL), F32)],
        compiler_params=_cparams(("parallel", "parallel", "arbitrary")),
        name="out_ffn",
    )(x2d, a, b_tm, c, w_out, norm_ffn, w_gate, w_up, w_down, norm_final)


def _prep_weights(w_in, w_out, s5_w_glu, gla_w_gate, w_ffn_gate, w_ffn_up, w_ffn_down):
    depth = w_in.shape[0]
    w_in_p = jnp.pad(w_in, ((0, 0), (0, 0), (0, IN_COLS_PAD - IN_COLS))).astype(BF16)
    wg_p = jnp.pad(gla_w_gate, ((0, 0), (0, LANE - GLA_GATE_RANK), (0, 0))).astype(BF16)
    del depth
    return (w_in_p, w_out.astype(BF16), s5_w_glu.astype(BF16), wg_p,
            w_ffn_gate.astype(BF16), w_ffn_up.astype(BF16), w_ffn_down.astype(BF16))


def _trunk(x, kv_pool, ssm_re0, ssm_im0, gla0, params):
    (norm_mix, w_in_p, w_out, ab, bb, cc, s5_d, w_glu, s5_b_glu, wg_p, gla_b_gate, gla_norm,
     norm_ffn, w_gate, w_up, w_down, norm_final) = params
    nseq, seqlen, _ = x.shape
    depth = w_in_p.shape[0]
    decode = kv_pool is not None
    if decode:
        assert seqlen == 1
        cache_k, cache_v, page_table = kv_pool
        n_pool = cache_k.shape[0]
        cache_k4 = cache_k.reshape(n_pool, depth, PAGE_SIZE, MOBA_WIDTH)
        cache_v4 = cache_v.reshape(n_pool, depth, PAGE_SIZE, MOBA_WIDTH)
        tw_nseq, tw_len = 1, nseq
    else:
        assert seqlen % MOBA_BLOCK == 0
        tw_nseq, tw_len = nseq, seqlen
    n = nseq * seqlen
    x2d = x.reshape(n, D_MODEL)
    k_rows, v_rows, s_re, s_im, s_gla = [], [], [], [], []
    for l in range(depth):
        (k, v, qh, kh, vh, u_tm, gq, gk, gv, gr, lf) = _in_proj(
            x2d, tw_nseq, tw_len, norm_mix[l][None], w_in_p[l], wg_p[l], gla_b_gate[l][None])
        k_rows.append(k)
        v_rows.append(v)
        if decode:
            q_tok = jnp.swapaxes(qh[0], 0, 1).reshape(nseq, 1, MOBA_WIDTH).astype(F32)
            a_out = _moba_decode(page_table, q_tok, k.reshape(nseq, 1, MOBA_WIDTH),
                                 v.reshape(nseq, 1, MOBA_WIDTH), cache_k4, cache_v4, l).reshape(n, MOBA_WIDTH)
        else:
            a_out = _moba_prompt(qh, kh, vh)
        if ssm_re0 is None:
            h0 = jnp.zeros((nseq, 2 * S5_CH), F32)
        else:
            h0 = jnp.concatenate([ssm_re0[l].reshape(nseq, S5_CH), ssm_im0[l].reshape(nseq, S5_CH)], axis=1)
        b_tm, h_fin = _s5(u_tm.reshape(n, S5_WIDTH), nseq, bb[l], ab[l], h0, cc[l], s5_d[l][None],
                          w_glu[l], s5_b_glu[l][None])
        s_re.append(h_fin[:, :S5_CH].reshape(nseq, S5_GROUPS, S5_STATE))
        s_im.append(h_fin[:, S5_CH:].reshape(nseq, S5_GROUPS, S5_STATE))
        b_tm = b_tm.reshape(tw_len, tw_nseq * S5_WIDTH)
        if decode:
            c_out, s_fin = _gla_decode(gq, gk, lf, gv, gr, gla0[l], gla_norm[l])
        else:
            c_out, s_t = _gla_prompt(gq, gk, gv, gr, lf, gla_norm[l][None], nseq, seqlen)
            s_fin = jnp.transpose(s_t.reshape(nseq, GLA_DV, GLA_HEADS, GLA_DK), (0, 2, 3, 1))
        s_gla.append(s_fin)
        x2d = _out_ffn(x2d, a_out, b_tm, c_out, tw_nseq, tw_len, w_out[l], norm_ffn[l][None],
                       w_gate[l], w_up[l], w_down[l], norm_final[None], final_norm=(l == depth - 1))
    shp = (nseq, seqlen, MOBA_HEADS, MOBA_HEAD_DIM)
    return (x2d.reshape(nseq, seqlen, D_MODEL),
            jnp.stack([k.reshape(shp) for k in k_rows], axis=1),
            jnp.stack([v.reshape(shp) for v in v_rows], axis=1),
            jnp.stack(s_re, axis=0), jnp.stack(s_im, axis=0), jnp.stack(s_gla, axis=0))


def kernel(x_prompt, x_sample, cache_k, cache_v, page_table, state_ssm_re, state_ssm_im, state_gla,
           norm_mix, w_in, w_out, s5_a_re, s5_a_im, s5_log_step, s5_b_re, s5_b_im, s5_c_re, s5_c_im,
           s5_d, s5_w_glu, s5_b_glu, gla_w_gate, gla_b_gate, gla_norm, norm_ffn, w_ffn_gate, w_ffn_up,
           w_ffn_down, norm_final):
    w_in_p, w_out16, w_glu16, wg_p, w_gate16, w_up16, w_down16 = _prep_weights(
        w_in, w_out, s5_w_glu, gla_w_gate, w_ffn_gate, w_ffn_up, w_ffn_down)
    ab, bb = _s5_discretise(s5_a_re, s5_a_im, s5_log_step, s5_b_re, s5_b_im)
    cc = _s5_out_map(s5_c_re, s5_c_im)
    params = (norm_mix, w_in_p, w_out16, ab, bb, cc, s5_d, w_glu16, s5_b_glu, wg_p, gla_b_gate, gla_norm,
              norm_ffn, w_gate16, w_up16, w_down16, norm_final)
    y_p, k_p, v_p, re_p, im_p, gla_p = _trunk(x_prompt, None, None, None, None, params)
    y_s, k_s, v_s, re_s, im_s, gla_s = _trunk(x_sample, (cache_k, cache_v, page_table),
                                              state_ssm_re, state_ssm_im, state_gla, params)
    return (y_p, y_s, k_p, v_p, k_s, v_s, re_p, im_p, re_s, im_s, gla_p, gla_s)
```

```python
import functools
import math

import jax
import jax.numpy as jnp
from jax import lax
from jax.experimental import pallas as pl
from jax.experimental.pallas import tpu as pltpu

F32 = jnp.float32
BF16 = jnp.bfloat16

D_MODEL = 1024
PAGE_SIZE = 128
MOBA_HEAD_DIM = 64
MOBA_HEADS = 8
MOBA_WIDTH = MOBA_HEADS * MOBA_HEAD_DIM
MOBA_BLOCK = 256
MOBA_TOPK = 3
S5_WIDTH = 256
S5_GROUP = 16
S5_GROUPS = 16
S5_STATE = 64
S5_CH = S5_GROUPS * S5_STATE
GLA_HEADS = 4
GLA_DK = 32
GLA_DV = 64
GLA_KEY_WIDTH = GLA_HEADS * GLA_DK
GLA_VAL_WIDTH = GLA_HEADS * GLA_DV
GLA_GATE_RANK = 16
GLA_GATE_NORM = 16.0
GLA_CHUNK = 64
FFN_HIDDEN = 2816
RMS_EPS = 1e-6
NEG_INF = float("-inf")

_OFF_Q = 0
_OFF_K = _OFF_Q + MOBA_WIDTH
_OFF_V = _OFF_K + MOBA_WIDTH
_OFF_U = _OFF_V + MOBA_WIDTH
_OFF_GQ = _OFF_U + S5_WIDTH
_OFF_GK = _OFF_GQ + GLA_KEY_WIDTH
_OFF_GV = _OFF_GK + GLA_KEY_WIDTH
_OFF_GR = _OFF_GV + GLA_VAL_WIDTH
_OFF_GLR = _OFF_GR + GLA_VAL_WIDTH
IN_COLS = _OFF_GLR + GLA_GATE_RANK
LANE = 128
IN_COLS_PAD = _OFF_GLR + LANE

VMEM_LIMIT = 56 * 1024 * 1024


def _cparams(sem):
    return pltpu.CompilerParams(dimension_semantics=sem, vmem_limit_bytes=VMEM_LIMIT)


def _nt(a, b):
    return lax.dot_general(a, b, (((1,), (1,)), ((), ())), preferred_element_type=F32)


def _tn(a, b):
    return lax.dot_general(a, b, (((0,), (0,)), ((), ())), preferred_element_type=F32)


def _mm(a, b):
    return jnp.dot(a, b, preferred_element_type=F32)


def _sigmoid(x):
    return 1.0 / (1.0 + jnp.exp(-x))


def _silu(x):
    return x * _sigmoid(x)


def _gelu_tanh(x):
    c = math.sqrt(2.0 / math.pi)
    return 0.5 * x * (1.0 + jnp.tanh(c * (x + 0.044715 * (x * x * x))))


def _log_sigmoid(x):
    return jnp.minimum(x, 0.0) - jnp.log(1.0 + jnp.exp(-jnp.abs(x)))


def _rms(x, g):
    return x * lax.rsqrt(jnp.mean(x * x, axis=-1, keepdims=True) + RMS_EPS) * g


def _in_proj_kernel(x_ref, g_ref, w_ref, wg_ref, bg_ref, k_ref, v_ref, *rest, head_major):
    h = _rms(x_ref[...], g_ref[...]).astype(BF16)

    def proj(lo, width):
        return _mm(h, w_ref[:, lo:lo + width])

    q = proj(_OFF_Q, MOBA_WIDTH)
    k = proj(_OFF_K, MOBA_WIDTH)
    v = proj(_OFF_V, MOBA_WIDTH)
    k_ref[...] = k
    v_ref[...] = v
    if head_major:
        qh_ref, kh_ref, vt_ref = rest[:3]
        rest = rest[3:]
        q16 = (q * (MOBA_HEAD_DIM ** -0.5)).astype(BF16)
        k16 = k.astype(BF16)
        vt16 = v.T.astype(BF16)
        for hd in range(MOBA_HEADS):
            sl = slice(hd * MOBA_HEAD_DIM, (hd + 1) * MOBA_HEAD_DIM)
            qh_ref[hd] = q16[:, sl]
            kh_ref[hd] = k16[:, sl]
            vt_ref[hd] = vt16[sl, :]
    else:
        rest[0][...] = q
        rest = rest[1:]
    u_ref, gq_ref, gk_ref, gv_ref, gr_ref, lf_ref = rest
    u_ref[...] = proj(_OFF_U, S5_WIDTH)
    gq_ref[...] = proj(_OFF_GQ, GLA_KEY_WIDTH) * (GLA_DK ** -0.5)
    gk_ref[...] = proj(_OFF_GK, GLA_KEY_WIDTH)
    gv_ref[...] = proj(_OFF_GV, GLA_VAL_WIDTH)
    gr_ref[...] = proj(_OFF_GR, GLA_VAL_WIDTH)
    glr = proj(_OFF_GLR, LANE)
    gate = _mm(glr.astype(BF16), wg_ref[...]) + bg_ref[...]
    lf_ref[...] = _log_sigmoid(gate) * (1.0 / GLA_GATE_NORM)


def _in_proj(x2d, nseq, seqlen, norm_g, w_pad, wg_pad, bg, head_major):
    n = nseq * seqlen
    tm = min(512, seqlen)
    nt = seqlen // tm
    tok = lambda width: pl.BlockSpec((tm, width), lambda b, t: (b * nt + t, 0))
    const = lambda shape: pl.BlockSpec(shape, lambda b, t: (0,) * len(shape))
    tokf = lambda width: jax.ShapeDtypeStruct((n, width), F32)
    if head_major:
        head = pl.BlockSpec((None, MOBA_HEADS, tm, MOBA_HEAD_DIM), lambda b, t: (b, 0, t, 0))
        head_t = pl.BlockSpec((None, MOBA_HEADS, MOBA_HEAD_DIM, tm), lambda b, t: (b, 0, 0, t))
        q_specs = [head, head, head_t]
        q_shapes = [jax.ShapeDtypeStruct((nseq, MOBA_HEADS, seqlen, MOBA_HEAD_DIM), BF16)] * 2 + [
            jax.ShapeDtypeStruct((nseq, MOBA_HEADS, MOBA_HEAD_DIM, seqlen), BF16)]
    else:
        q_specs = [tok(MOBA_WIDTH)]
        q_shapes = [tokf(MOBA_WIDTH)]
    return pl.pallas_call(
        functools.partial(_in_proj_kernel, head_major=head_major),
        grid=(nseq, nt),
        in_specs=[tok(D_MODEL), const((1, D_MODEL)), const((D_MODEL, IN_COLS_PAD)),
                  const((LANE, GLA_KEY_WIDTH)), const((1, GLA_KEY_WIDTH))],
        out_specs=[tok(MOBA_WIDTH), tok(MOBA_WIDTH)] + q_specs + [
            pl.BlockSpec((tm, S5_WIDTH), lambda b, t: (t, b)),
            tok(GLA_KEY_WIDTH), tok(GLA_KEY_WIDTH), tok(GLA_VAL_WIDTH), tok(GLA_VAL_WIDTH),
            tok(GLA_KEY_WIDTH)],
        out_shape=[tokf(MOBA_WIDTH), tokf(MOBA_WIDTH)] + q_shapes + [
            jax.ShapeDtypeStruct((seqlen, nseq * S5_WIDTH), F32),
            tokf(GLA_KEY_WIDTH), tokf(GLA_KEY_WIDTH), tokf(GLA_VAL_WIDTH), tokf(GLA_VAL_WIDTH),
            tokf(GLA_KEY_WIDTH)],
        compiler_params=_cparams(("parallel", "parallel")),
        name="in_proj",
    )(x2d, norm_g, w_pad, wg_pad, bg)


def _moba_prompt_kernel(q_ref, k_ref, vt_ref, o_ref, means_ref, bias_ref, m_ref, l_ref, acc_ref,
                        *, nblk, nblk_pad):
    i = pl.program_id(2)
    bs = MOBA_BLOCK
    tile = 2 * bs

    @pl.when(i == 0)
    def _():
        means_ref[...] = jnp.zeros_like(means_ref)
        for hh in range(2):
            for j in range(nblk):
                kj = k_ref[hh, j * bs:(j + 1) * bs, :].astype(F32)
                means_ref[hh, j:j + 1, :] = jnp.mean(kj, axis=0, keepdims=True)

    blk_iota = lax.broadcasted_iota(jnp.int32, (nblk_pad, bs), 0)
    key_iota = lax.broadcasted_iota(jnp.int32, (bs, bs), 0)
    qry_iota = lax.broadcasted_iota(jnp.int32, (bs, bs), 1)
    causal_bias = jnp.where(key_iota <= qry_iota, 0.0, NEG_INF)
    own_first = (i % 2) == 0
    t_own = i // 2
    r_own = pl.multiple_of(t_own * tile, tile)

    for hh in range(2):
        qb = q_ref[hh]
        gate = _nt(means_ref[hh].astype(BF16), qb)
        gate = jnp.where(blk_iota < i, gate, NEG_INF)
        for j in range(nblk):
            gj = gate[j:j + 1, :]
            ahead = (gate > gj) | ((gate == gj) & (blk_iota < j))
            cnt = jnp.sum(ahead.astype(F32), axis=0, keepdims=True)
            past_bias = jnp.where(j < i, 0.0, NEG_INF)
            bias_ref[hh, j:j + 1, :] = jnp.where(cnt < MOBA_TOPK, past_bias, NEG_INF)

        s = _nt(k_ref[hh, pl.ds(r_own, tile), :], qb)
        prev = jnp.broadcast_to(bias_ref[hh, pl.ds(2 * t_own, 1), :], (bs, bs))
        first = jnp.where(own_first, causal_bias, prev)
        second = jnp.where(own_first, NEG_INF, causal_bias)
        s = s + jnp.concatenate([first, second], axis=0)
        m = jnp.max(s, axis=0, keepdims=True)
        p = jnp.exp(s - m)
        m_ref[hh] = m
        l_ref[hh] = jnp.sum(p, axis=0, keepdims=True)
        acc_ref[hh] = _mm(vt_ref[hh, :, pl.ds(r_own, tile)], p.astype(BF16))

    def body(t, carry):
        c0 = pl.multiple_of(t * tile, tile)
        for hh in range(2):
            s = _nt(k_ref[hh, pl.ds(c0, tile), :], q_ref[hh])
            s = jnp.concatenate([s[:bs] + bias_ref[hh, pl.ds(2 * t, 1), :],
                                 s[bs:] + bias_ref[hh, pl.ds(2 * t + 1, 1), :]], axis=0)
            m = m_ref[hh]
            m_new = jnp.maximum(m, jnp.max(s, axis=0, keepdims=True))
            alpha = jnp.exp(m - m_new)
            p = jnp.exp(s - m_new)
            l_ref[hh] = alpha * l_ref[hh] + jnp.sum(p, axis=0, keepdims=True)
            acc_ref[hh] = alpha * acc_ref[hh] + _mm(vt_ref[hh, :, pl.ds(c0, tile)], p.astype(BF16))
            m_ref[hh] = m_new
        return carry

    lax.fori_loop(0, t_own, body, 0)
    out = jnp.concatenate([acc_ref[0] / l_ref[0], acc_ref[1] / l_ref[1]], axis=0)
    o_ref[...] = out.T


def _moba_prompt(qh, kh, vt):
    nseq, _, seqlen, dh = qh.shape
    bs = MOBA_BLOCK
    assert seqlen % (2 * bs) == 0
    nblk = seqlen // bs
    nblk_pad = -(-nblk // 8) * 8
    kern = functools.partial(_moba_prompt_kernel, nblk=nblk, nblk_pad=nblk_pad)
    return pl.pallas_call(
        kern,
        grid=(nseq, MOBA_HEADS // 2, nblk),
        in_specs=[pl.BlockSpec((None, 2, bs, dh), lambda b, h, i: (b, h, i, 0)),
                  pl.BlockSpec((None, 2, seqlen, dh), lambda b, h, i: (b, h, 0, 0)),
                  pl.BlockSpec((None, 2, dh, seqlen), lambda b, h, i: (b, h, 0, 0))],
        out_specs=pl.BlockSpec((bs, 2 * dh), lambda b, h, i: (b * nblk + i, h)),
        out_shape=jax.ShapeDtypeStruct((nseq * seqlen, MOBA_WIDTH), F32),
        scratch_shapes=[pltpu.VMEM((2, nblk_pad, dh), F32), pltpu.VMEM((2, nblk_pad, bs), F32),
                        pltpu.VMEM((2, 1, bs), F32), pltpu.VMEM((2, 1, bs), F32),
                        pltpu.VMEM((2, dh, bs), F32)],
        compiler_params=_cparams(("parallel", "parallel", "arbitrary")),
        name="moba_prompt",
    )(qh, kh, vt)


def _moba_decode_kernel(pt_ref, q_ref, kn_ref, vn_ref, *rest, ppg, nblk):
    del pt_ref
    k_refs = rest[:ppg]
    v_refs = rest[ppg:2 * ppg]
    o_ref = rest[2 * ppg]
    g_ref, m_ref, l_ref, ob_ref = rest[2 * ppg + 1:]
    s_id = pl.program_id(1)
    nsteps = pl.num_programs(1)
    scale = MOBA_HEAD_DIM ** -0.5
    rows = PAGE_SIZE * MOBA_HEADS

    q = q_ref[...]
    q16 = (q * scale).astype(BF16)
    col_head = lax.broadcasted_iota(jnp.int32, (MOBA_HEADS, 2 * rows), 1) & (MOBA_HEADS - 1)
    row_head = lax.broadcasted_iota(jnp.int32, (MOBA_HEADS, 2 * rows), 0)
    own = col_head == row_head
    lane_blk = lax.broadcasted_iota(jnp.int32, (MOBA_HEADS, LANE), 1)

    @pl.when(s_id == 0)
    def _():
        g_ref[...] = jnp.full_like(g_ref, NEG_INF)
        m_ref[...] = jnp.zeros_like(m_ref)
        l_ref[...] = jnp.zeros_like(l_ref)

    for bi in range(ppg // 2):
        k0 = k_refs[2 * bi][...]
        k1 = k_refs[2 * bi + 1][...]
        n = s_id * (ppg // 2) + bi
        mean = (jnp.sum(k0, axis=0) + jnp.sum(k1, axis=0)) * (1.0 / MOBA_BLOCK)
        gate = jnp.sum(q * mean, axis=1, keepdims=True)
        kb = jnp.concatenate([k0.reshape(rows, MOBA_HEAD_DIM), k1.reshape(rows, MOBA_HEAD_DIM)], axis=0)
        vb = jnp.concatenate([v_refs[2 * bi][...].reshape(rows, MOBA_HEAD_DIM),
                              v_refs[2 * bi + 1][...].reshape(rows, MOBA_HEAD_DIM)], axis=0)
        s = jnp.where(own, _nt(q16, kb.astype(BF16)), NEG_INF)
        mb = jnp.max(s, axis=1, keepdims=True)
        p = jnp.exp(s - mb)
        lb = jnp.sum(p, axis=1, keepdims=True)
        ob_ref[n] = _mm(p.astype(BF16), vb.astype(BF16))
        hit = lane_blk == n
        g_ref[...] = jnp.where(hit, gate, g_ref[...])
        m_ref[...] = jnp.where(hit, mb, m_ref[...])
        l_ref[...] = jnp.where(hit, lb, l_ref[...])

    @pl.when(s_id == nsteps - 1)
    def _():
        g = g_ref[...]
        sel = jnp.zeros(g.shape, jnp.bool_)
        lane_f = lane_blk.astype(F32)
        for _ in range(MOBA_TOPK):
            mx = jnp.max(g, axis=1, keepdims=True)
            idx = jnp.min(jnp.where(g == mx, lane_f, float(LANE)), axis=1, keepdims=True)
            pick = lane_f == idx
            sel = sel | pick
            g = jnp.where(pick, NEG_INF, g)
        s_own = jnp.sum(q * kn_ref[...], axis=1, keepdims=True) * scale
        m_all = m_ref[...]
        mx = jnp.maximum(jnp.max(jnp.where(sel, m_all, NEG_INF), axis=1, keepdims=True), s_own)
        w = jnp.where(sel, jnp.exp(m_all - mx), 0.0)
        w_own = jnp.exp(s_own - mx)
        den = jnp.sum(w * l_ref[...], axis=1, keepdims=True) + w_own

        def add_block(n, num):
            wn = jnp.sum(jnp.where(lane_blk == n, w, 0.0), axis=1, keepdims=True)
            return num + wn * ob_ref[n]

        num = lax.fori_loop(0, nblk, add_block, w_own * vn_ref[...])
        o_ref[...] = num / den


def _moba_decode(page_table, q, k_new, v_new, cache_k, cache_v, layer):
    nseq, n_pages = page_table.shape
    nblk = n_pages * PAGE_SIZE // MOBA_BLOCK
    assert n_pages * PAGE_SIZE == nblk * MOBA_BLOCK and MOBA_TOPK <= nblk <= LANE
    ppg = 8 if n_pages % 8 == 0 else 2
    nsteps = n_pages // ppg
    row = pl.BlockSpec((None, MOBA_HEADS, MOBA_HEAD_DIM), lambda b, s, pt: (b, 0, 0))

    def page_spec(p):
        return pl.BlockSpec((None, None, PAGE_SIZE, MOBA_HEADS, MOBA_HEAD_DIM),
                            lambda b, s, pt, p=p: (pt[b, s * ppg + p], layer, 0, 0, 0))

    grid_spec = pltpu.PrefetchScalarGridSpec(
        num_scalar_prefetch=1,
        grid=(nseq, nsteps),
        in_specs=[row, row, row] + [page_spec(p) for p in range(ppg)] * 2,
        out_specs=row,
        scratch_shapes=[pltpu.VMEM((MOBA_HEADS, LANE), F32)] * 3
        + [pltpu.VMEM((nblk, MOBA_HEADS, MOBA_HEAD_DIM), F32)],
    )
    kern = functools.partial(_moba_decode_kernel, ppg=ppg, nblk=nblk)
    return pl.pallas_call(
        kern,
        grid_spec=grid_spec,
        out_shape=jax.ShapeDtypeStruct((nseq, MOBA_HEADS, MOBA_HEAD_DIM), F32),
        compiler_params=_cparams(("parallel", "arbitrary")),
        name="moba_decode",
    )(page_table, q, k_new, v_new, *([cache_k] * ppg), *([cache_v] * ppg))


def _s5_disc_kernel(are_ref, aim_ref, ls_ref, bre_ref, bim_ref, abr_ref, abi_ref, bbr_ref, bbi_ref):
    a_re = are_ref[...]
    a_im = aim_ref[...]
    step = jnp.exp(ls_ref[...])
    mag = jnp.exp(a_re * step)
    ab_re = mag * jnp.cos(a_im * step)
    ab_im = mag * jnp.sin(a_im * step)
    den = a_re * a_re + a_im * a_im
    n_re = ab_re - 1.0
    f_re = (n_re * a_re + ab_im * a_im) / den
    f_im = (ab_im * a_re - n_re * a_im) / den
    b_re = bre_ref[...]
    b_im = bim_ref[...]
    abr_ref[...] = ab_re
    abi_ref[...] = ab_im
    bbr_ref[...] = f_re * b_re - f_im * b_im
    bbi_ref[...] = f_re * b_im + f_im * b_re


def _s5_discretise(a_re, a_im, log_step, b_re, b_im):
    depth = a_re.shape[0]
    rows = depth * S5_GROUPS * S5_GROUP
    rep = lambda a: jnp.broadcast_to(a[:, :, None, :], (depth, S5_GROUPS, S5_GROUP, S5_STATE)).reshape(rows, S5_STATE)
    ls = jnp.broadcast_to(log_step[:, :, None, None], (depth, S5_GROUPS, S5_GROUP, S5_STATE)).reshape(rows, S5_STATE)
    bt = lambda b: jnp.swapaxes(b, 2, 3).reshape(rows, S5_STATE)
    shp = jax.ShapeDtypeStruct((rows, S5_STATE), F32)
    abr, abi, bbr, bbi = pl.pallas_call(
        _s5_disc_kernel, out_shape=[shp] * 4, name="s5_disc",
    )(rep(a_re), rep(a_im), ls, bt(b_re), bt(b_im))
    r4 = lambda a: a.reshape(depth, S5_GROUPS, S5_GROUP, S5_STATE)
    ab = jnp.concatenate([r4(abr)[:, :, 0].reshape(depth, 1, S5_CH),
                          r4(abi)[:, :, 0].reshape(depth, 1, S5_CH)], axis=1)
    eye = jnp.eye(S5_GROUPS, dtype=F32)
    bd = lambda a: (r4(a)[:, :, :, None, :] * eye[None, :, None, :, None]).reshape(depth, S5_WIDTH, S5_CH)
    bb = jnp.concatenate([bd(bbr), bd(bbi)], axis=2)
    return ab, bb.astype(BF16)


def _s5_out_map(c_re, c_im):
    depth = c_re.shape[0]
    eye = jnp.eye(S5_GROUPS, dtype=F32)
    bd = lambda c: (jnp.swapaxes(c, 2, 3)[:, :, :, None, :] * eye[None, :, None, :, None]).reshape(depth, S5_CH, S5_WIDTH)
    return jnp.concatenate([bd(c_re), -bd(c_im)], axis=1).astype(BF16)


S5_SCAN_LANES = 512


def _s5_kernel(u_ref, bb_ref, ab_ref, h0_ref, cc_ref, d_ref, wglu_ref, bglu_ref,
               o_ref, hfin_ref, xs_ref, hc_ref, *, nb, tt):
    @pl.when(pl.program_id(0) == 0)
    def _():
        hc_ref[...] = h0_ref[...]

    u = u_ref[...]
    xs_ref[...] = _mm(u.astype(BF16), bb_ref[...])

    for c0 in range(0, S5_CH, S5_SCAN_LANES):
        re_sl = slice(c0, c0 + S5_SCAN_LANES)
        im_sl = slice(S5_CH + c0, S5_CH + c0 + S5_SCAN_LANES)
        ar = jnp.broadcast_to(ab_ref[0:1, re_sl], (nb, S5_SCAN_LANES))
        ai = jnp.broadcast_to(ab_ref[1:2, re_sl], (nb, S5_SCAN_LANES))

        def step(t, carry, re_sl=re_sl, im_sl=im_sl, ar=ar, ai=ai):
            hr, hi = carry
            rows = pl.ds(pl.multiple_of(t * nb, nb), nb)
            nhr = ar * hr - ai * hi + xs_ref[rows, re_sl]
            nhi = ar * hi + ai * hr + xs_ref[rows, im_sl]
            xs_ref[rows, re_sl] = nhr
            xs_ref[rows, im_sl] = nhi
            return nhr, nhi

        hr, hi = lax.fori_loop(0, tt, step, (hc_ref[:, re_sl], hc_ref[:, im_sl]), unroll=min(4, tt))
        hc_ref[:, re_sl] = hr
        hc_ref[:, im_sl] = hi

    y = _mm(xs_ref[...].astype(BF16), cc_ref[...]) + d_ref[...] * u
    yg = _gelu_tanh(y)
    gate = _sigmoid(_mm(yg.astype(BF16), wglu_ref[...]) + bglu_ref[...])
    o_ref[...] = yg * gate
    hfin_ref[...] = hc_ref[...]


def _s5(u_tm, nb, bb, ab, h0, cc, d, wglu, bglu):
    rows = u_tm.shape[0]
    t_total = rows // nb
    tt = min(64, t_total)
    const = lambda shape: pl.BlockSpec(shape, lambda t: (0,) * len(shape))
    kern = functools.partial(_s5_kernel, nb=nb, tt=tt)
    return pl.pallas_call(
        kern,
        grid=(t_total // tt,),
        in_specs=[pl.BlockSpec((tt * nb, S5_WIDTH), lambda t: (t, 0)),
                  const((S5_WIDTH, 2 * S5_CH)), const((2, S5_CH)), const((nb, 2 * S5_CH)),
                  const((2 * S5_CH, S5_WIDTH)), const((1, S5_WIDTH)),
                  const((S5_WIDTH, S5_WIDTH)), const((1, S5_WIDTH))],
        out_specs=[pl.BlockSpec((tt * nb, S5_WIDTH), lambda t: (t, 0)), const((nb, 2 * S5_CH))],
        out_shape=[jax.ShapeDtypeStruct((rows, S5_WIDTH), F32), jax.ShapeDtypeStruct((nb, 2 * S5_CH), F32)],
        scratch_shapes=[pltpu.VMEM((tt * nb, 2 * S5_CH), F32), pltpu.VMEM((nb, 2 * S5_CH), F32)],
        compiler_params=_cparams(("arbitrary",)),
        name="s5_scan",
    )(u_tm, bb, ab, h0, cc, d, wglu, bglu)


def _split3(x):
    hi = x.astype(BF16)
    r = x - hi.astype(F32)
    mid = r.astype(BF16)
    lo = (r - mid.astype(F32)).astype(BF16)
    return hi, mid, lo


def _gla_prompt_kernel(gq_ref, gk_ref, gv_ref, gr_ref, lf_ref, gn_ref, o_ref, sfin_ref, st_ref, *, nchunk):
    c = GLA_CHUNK

    @pl.when(pl.program_id(1) == 0)
    def _():
        st_ref[...] = jnp.zeros_like(st_ref)

    row = lax.broadcasted_iota(jnp.int32, (c, c), 0)
    col = lax.broadcasted_iota(jnp.int32, (c, c), 1)
    causal = col <= row
    tri = causal.astype(BF16)
    gn = gn_ref[...]

    for ci in range(nchunk):
        rows = slice(ci * c, (ci + 1) * c)
        g_hi, g_mid, g_lo = _split3(lf_ref[rows, :])
        bcum = _mm(tri, g_hi) + _mm(tri, g_mid) + _mm(tri, g_lo)
        blast = bcum[c - 1:c, :]
        qd = (gq_ref[rows, :] * jnp.exp(bcum)).astype(BF16)
        kd = (gk_ref[rows, :] * jnp.exp(-bcum)).astype(BF16)
        kt = (gk_ref[rows, :] * jnp.exp(blast - bcum)).astype(BF16)
        st = st_ref[...]
        st_ref[...] = st * jnp.exp(blast)
        st16 = st.astype(BF16)
        for hd in range(GLA_HEADS):
            ks = slice(hd * GLA_DK, (hd + 1) * GLA_DK)
            vs = slice(hd * GLA_DV, (hd + 1) * GLA_DV)
            vh = gv_ref[rows, vs].astype(BF16)
            att = jnp.where(causal, _nt(qd[:, ks], kd[:, ks]), 0.0)
            o = _nt(qd[:, ks], st16[:, ks]) + _mm(att.astype(BF16), vh)
            st_ref[:, ks] += _tn(vh, kt[:, ks])
            o_ref[rows, vs] = _rms(o, gn) * _silu(gr_ref[rows, vs])

    sfin_ref[...] = st_ref[...]


def _gla_prompt(gq, gk, gv, gr, lf, gnorm, nseq, seqlen):
    assert seqlen % GLA_CHUNK == 0
    tt = min(256, seqlen)
    nt = seqlen // tt
    tok = lambda width: pl.BlockSpec((tt, width), lambda b, t: (b * nt + t, 0))
    kern = functools.partial(_gla_prompt_kernel, nchunk=tt // GLA_CHUNK)
    return pl.pallas_call(
        kern,
        grid=(nseq, nt),
        in_specs=[tok(GLA_KEY_WIDTH), tok(GLA_KEY_WIDTH), tok(GLA_VAL_WIDTH), tok(GLA_VAL_WIDTH),
                  tok(GLA_KEY_WIDTH), pl.BlockSpec((1, GLA_DV), lambda b, t: (0, 0))],
        out_specs=[tok(GLA_VAL_WIDTH),
                   pl.BlockSpec((None, GLA_DV, GLA_KEY_WIDTH), lambda b, t: (b, 0, 0))],
        out_shape=[jax.ShapeDtypeStruct((nseq * seqlen, GLA_VAL_WIDTH), F32),
                   jax.ShapeDtypeStruct((nseq, GLA_DV, GLA_KEY_WIDTH), F32)],
        scratch_shapes=[pltpu.VMEM((GLA_DV, GLA_KEY_WIDTH), F32)],
        compiler_params=_cparams(("parallel", "arbitrary")),
        name="gla_prompt",
    )(gq, gk, gv, gr, lf, gnorm)


def _gla_decode_kernel(q_ref, k_ref, g_ref, v_ref, r_ref, s0_ref, gn_ref, o_ref, s_ref):
    s = jnp.exp(g_ref[...]) * s0_ref[...] + k_ref[...] * v_ref[...]
    s_ref[...] = s
    o = jnp.sum(q_ref[...] * s, axis=1, keepdims=True)
    o_ref[...] = _rms(o, gn_ref[...]) * _silu(r_ref[...])


def _gla_decode(gq, gk, lf, gv, gr, s0, gnorm):
    n = gq.shape[0]
    col = lambda a: a.reshape(n, GLA_HEADS, GLA_DK, 1)
    rowv = lambda a: a.reshape(n, GLA_HEADS, 1, GLA_DV)
    cspec = pl.BlockSpec((None, GLA_HEADS, GLA_DK, 1), lambda b: (b, 0, 0, 0))
    rspec = pl.BlockSpec((None, GLA_HEADS, 1, GLA_DV), lambda b: (b, 0, 0, 0))
    sspec = pl.BlockSpec((None, GLA_HEADS, GLA_DK, GLA_DV), lambda b: (b, 0, 0, 0))
    o, s = pl.pallas_call(
        _gla_decode_kernel,
        grid=(n,),
        in_specs=[cspec, cspec, cspec, rspec, rspec, sspec,
                  pl.BlockSpec((1, 1, GLA_DV), lambda b: (0, 0, 0))],
        out_specs=[rspec, sspec],
        out_shape=[jax.ShapeDtypeStruct((n, GLA_HEADS, 1, GLA_DV), F32),
                   jax.ShapeDtypeStruct((n, GLA_HEADS, GLA_DK, GLA_DV), F32)],
        compiler_params=_cparams(("parallel",)),
        name="gla_decode",
    )(col(gq), col(gk), col(lf), rowv(gv), rowv(gr), s0, gnorm.reshape(1, 1, GLA_DV))
    return o.reshape(n, GLA_VAL_WIDTH), s


def _out_ffn_kernel(x_ref, a_ref, b_ref, c_ref, wo_ref, nf_ref, wg_ref, wu_ref, wd_ref, nfin_ref,
                    o_ref, hf_ref, acc_ref, *, final_norm):
    j = pl.program_id(2)

    @pl.when(j == 0)
    def _():
        mix = jnp.concatenate([a_ref[...], b_ref[...], c_ref[...]], axis=1).astype(BF16)
        x1 = x_ref[...] + _mm(mix, wo_ref[...])
        acc_ref[...] = x1
        hf_ref[...] = _rms(x1, nf_ref[...]).astype(BF16)

    hf = hf_ref[...]
    act = _silu(_mm(hf, wg_ref[...])) * _mm(hf, wu_ref[...])
    acc_ref[...] += _mm(act.astype(BF16), wd_ref[...])

    @pl.when(j == pl.num_programs(2) - 1)
    def _():
        y = acc_ref[...]
        o_ref[...] = _rms(y, nfin_ref[...]) if final_norm else y


def _out_ffn(x2d, a, b_tm, c, nseq, seqlen, w_out, norm_ffn, w_gate, w_up, w_down, norm_final, final_norm):
    tm = min(512, seqlen)
    nt = seqlen // tm
    th = 256
    nh = FFN_HIDDEN // th
    tok = lambda width: pl.BlockSpec((tm, width), lambda b, t, j: (b * nt + t, 0))
    const = lambda shape: pl.BlockSpec(shape, lambda b, t, j: (0,) * len(shape))
    kern = functools.partial(_out_ffn_kernel, final_norm=final_norm)
    return pl.pallas_call(
        kern,
        grid=(nseq, nt, nh),
        in_specs=[tok(D_MODEL), tok(MOBA_WIDTH),
                  pl.BlockSpec((tm, S5_WIDTH), lambda b, t, j: (t, b)),
                  tok(GLA_VAL_WIDTH), const((D_MODEL, D_MODEL)), const((1, D_MODEL)),
                  pl.BlockSpec((D_MODEL, th), lambda b, t, j: (0, j)),
                  pl.BlockSpec((D_MODEL, th), lambda b, t, j: (0, j)),
                  pl.BlockSpec((th, D_MODEL), lambda b, t, j: (j, 0)),
                  const((1, D_MODEL))],
        out_specs=tok(D_MODEL),
        out_shape=jax.ShapeDtypeStruct((nseq * seqlen, D_MODEL), F32),
        scratch_shapes=[pltpu.VMEM((tm, D_MODEL), BF16), pltpu.VMEM((tm, D_MODEL), F32)],
        compiler_params=_cparams(("parallel", "parallel", "arbitrary")),
        name="out_ffn",
    )(x2d, a, b_tm, c, w_out, norm_ffn, w_gate, w_up, w_down, norm_final)


def _prep_weights(w_in, w_out, s5_w_glu, gla_w_gate, w_ffn_gate, w_ffn_up, w_ffn_down):
    depth = w_in.shape[0]
    w_in_p = jnp.pad(w_in, ((0, 0), (0, 0), (0, IN_COLS_PAD - IN_COLS))).astype(BF16)
    wg_p = jnp.pad(gla_w_gate, ((0, 0), (0, LANE - GLA_GATE_RANK), (0, 0))).astype(BF16)
    del depth
    return (w_in_p, w_out.astype(BF16), s5_w_glu.astype(BF16), wg_p,
            w_ffn_gate.astype(BF16), w_ffn_up.astype(BF16), w_ffn_down.astype(BF16))


def _trunk(x, kv_pool, ssm_re0, ssm_im0, gla0, params):
    (norm_mix, w_in_p, w_out, ab, bb, cc, s5_d, w_glu, s5_b_glu, wg_p, gla_b_gate, gla_norm,
     norm_ffn, w_gate, w_up, w_down, norm_final) = params
    nseq, seqlen, _ = x.shape
    depth = w_in_p.shape[0]
    decode = kv_pool is not None
    if decode:
        assert seqlen == 1
        cache_k, cache_v, page_table = kv_pool
        tw_nseq, tw_len = 1, nseq
    else:
        assert seqlen % MOBA_BLOCK == 0
        tw_nseq, tw_len = nseq, seqlen
    n = nseq * seqlen
    x2d = x.reshape(n, D_MODEL)
    k_rows, v_rows, s_re, s_im, s_gla = [], [], [], [], []
    for l in range(depth):
        k, v, *q_parts, u_tm, gq, gk, gv, gr, lf = _in_proj(
            x2d, tw_nseq, tw_len, norm_mix[l][None], w_in_p[l], wg_p[l], gla_b_gate[l][None],
            head_major=not decode)
        k_rows.append(k)
        v_rows.append(v)
        if decode:
            heads = lambda a: a.reshape(nseq, MOBA_HEADS, MOBA_HEAD_DIM)
            a_out = _moba_decode(page_table, heads(q_parts[0]), heads(k), heads(v),
                                 cache_k, cache_v, l).reshape(n, MOBA_WIDTH)
        else:
            a_out = _moba_prompt(*q_parts)
        if ssm_re0 is None:
            h0 = jnp.zeros((nseq, 2 * S5_CH), F32)
        else:
            h0 = jnp.concatenate([ssm_re0[l].reshape(nseq, S5_CH), ssm_im0[l].reshape(nseq, S5_CH)], axis=1)
        b_tm, h_fin = _s5(u_tm.reshape(n, S5_WIDTH), nseq, bb[l], ab[l], h0, cc[l], s5_d[l][None],
                          w_glu[l], s5_b_glu[l][None])
        s_re.append(h_fin[:, :S5_CH].reshape(nseq, S5_GROUPS, S5_STATE))
        s_im.append(h_fin[:, S5_CH:].reshape(nseq, S5_GROUPS, S5_STATE))
        b_tm = b_tm.reshape(tw_len, tw_nseq * S5_WIDTH)
        if decode:
            c_out, s_fin = _gla_decode(gq, gk, lf, gv, gr, gla0[l], gla_norm[l])
        else:
            c_out, s_t = _gla_prompt(gq, gk, gv, gr, lf, gla_norm[l][None], nseq, seqlen)
            s_fin = jnp.transpose(s_t.reshape(nseq, GLA_DV, GLA_HEADS, GLA_DK), (0, 2, 3, 1))
        s_gla.append(s_fin)
        x2d = _out_ffn(x2d, a_out, b_tm, c_out, tw_nseq, tw_len, w_out[l], norm_ffn[l][None],
                       w_gate[l], w_up[l], w_down[l], norm_final[None], final_norm=(l == depth - 1))
    shp = (nseq, seqlen, MOBA_HEADS, MOBA_HEAD_DIM)
    return (x2d.reshape(nseq, seqlen, D_MODEL),
            jnp.stack([k.reshape(shp) for k in k_rows], axis=1),
            jnp.stack([v.reshape(shp) for v in v_rows], axis=1),
            jnp.stack(s_re, axis=0), jnp.stack(s_im, axis=0), jnp.stack(s_gla, axis=0))


def kernel(x_prompt, x_sample, cache_k, cache_v, page_table, state_ssm_re, state_ssm_im, state_gla,
           norm_mix, w_in, w_out, s5_a_re, s5_a_im, s5_log_step, s5_b_re, s5_b_im, s5_c_re, s5_c_im,
           s5_d, s5_w_glu, s5_b_glu, gla_w_gate, gla_b_gate, gla_norm, norm_ffn, w_ffn_gate, w_ffn_up,
           w_ffn_down, norm_final):
    w_in_p, w_out16, w_glu16, wg_p, w_gate16, w_up16, w_down16 = _prep_weights(
        w_in, w_out, s5_w_glu, gla_w_gate, w_ffn_gate, w_ffn_up, w_ffn_down)
    ab, bb = _s5_discretise(s5_a_re, s5_a_im, s5_log_step, s5_b_re, s5_b_im)
    cc = _s5_out_map(s5_c_re, s5_c_im)
    params = (norm_mix, w_in_p, w_out16, ab, bb, cc, s5_d, w_glu16, s5_b_glu, wg_p, gla_b_gate, gla_norm,
              norm_ffn, w_gate16, w_up16, w_down16, norm_final)
    y_p, k_p, v_p, re_p, im_p, gla_p = _trunk(x_prompt, None, None, None, None, params)
    y_s, k_s, v_s, re_s, im_s, gla_s = _trunk(x_sample, (cache_k, cache_v, page_table),
                                              state_ssm_re, state_ssm_im, state_gla, params)
    return (y_p, y_s, k_p, v_p, k_s, v_s, re_p, im_p, re_s, im_s, gla_p, gla_s)
```

```python
import functools
import math

import jax
import jax.numpy as jnp
from jax import lax
from jax.experimental import pallas as pl
from jax.experimental.pallas import tpu as pltpu

F32 = jnp.float32
BF16 = jnp.bfloat16

D_MODEL = 1024
PAGE_SIZE = 128
MOBA_HEAD_DIM = 64
MOBA_HEADS = 8
MOBA_WIDTH = MOBA_HEADS * MOBA_HEAD_DIM
MOBA_BLOCK = 256
MOBA_TOPK = 3
S5_WIDTH = 256
S5_GROUP = 16
S5_GROUPS = 16
S5_STATE = 64
S5_CH = S5_GROUPS * S5_STATE
GLA_HEADS = 4
GLA_DK = 32
GLA_DV = 64
GLA_KEY_WIDTH = GLA_HEADS * GLA_DK
GLA_VAL_WIDTH = GLA_HEADS * GLA_DV
GLA_GATE_RANK = 16
GLA_GATE_NORM = 16.0
GLA_CHUNK = 64
FFN_HIDDEN = 2816
RMS_EPS = 1e-6
NEG_INF = float("-inf")

_OFF_Q = 0
_OFF_K = _OFF_Q + MOBA_WIDTH
_OFF_V = _OFF_K + MOBA_WIDTH
_OFF_U = _OFF_V + MOBA_WIDTH
_OFF_GQ = _OFF_U + S5_WIDTH
_OFF_GK = _OFF_GQ + GLA_KEY_WIDTH
_OFF_GV = _OFF_GK + GLA_KEY_WIDTH
_OFF_GR = _OFF_GV + GLA_VAL_WIDTH
_OFF_GLR = _OFF_GR + GLA_VAL_WIDTH
IN_COLS = _OFF_GLR + GLA_GATE_RANK
LANE = 128
IN_COLS_PAD = _OFF_GLR + LANE

VMEM_LIMIT = 56 * 1024 * 1024


def _cparams(sem):
    return pltpu.CompilerParams(dimension_semantics=sem, vmem_limit_bytes=VMEM_LIMIT)


def _nt(a, b):
    return lax.dot_general(a, b, (((1,), (1,)), ((), ())), preferred_element_type=F32)


def _tn(a, b):
    return lax.dot_general(a, b, (((0,), (0,)), ((), ())), preferred_element_type=F32)


def _mm(a, b):
    return jnp.dot(a, b, preferred_element_type=F32)


def _sigmoid(x):
    return 1.0 / (1.0 + jnp.exp(-x))


def _silu(x):
    return x * _sigmoid(x)


def _gelu_tanh(x):
    c = math.sqrt(2.0 / math.pi)
    return 0.5 * x * (1.0 + jnp.tanh(c * (x + 0.044715 * (x * x * x))))


def _log_sigmoid(x):
    return jnp.minimum(x, 0.0) - jnp.log(1.0 + jnp.exp(-jnp.abs(x)))


def _rms(x, g):
    return x * lax.rsqrt(jnp.mean(x * x, axis=-1, keepdims=True) + RMS_EPS) * g


def _in_proj_kernel(x_ref, g_ref, w_ref, wg_ref, bg_ref, k_ref, v_ref, *rest, head_major):
    h = _rms(x_ref[...], g_ref[...]).astype(BF16)

    def proj(lo, width):
        return _mm(h, w_ref[:, lo:lo + width])

    q = proj(_OFF_Q, MOBA_WIDTH)
    k = proj(_OFF_K, MOBA_WIDTH)
    v = proj(_OFF_V, MOBA_WIDTH)
    k_ref[...] = k
    v_ref[...] = v
    if head_major:
        qh_ref, kh_ref, vt_ref = rest[:3]
        rest = rest[3:]
        q16 = (q * (MOBA_HEAD_DIM ** -0.5)).astype(BF16)
        k16 = k.astype(BF16)
        vt16 = v.T.astype(BF16)
        for hd in range(MOBA_HEADS):
            sl = slice(hd * MOBA_HEAD_DIM, (hd + 1) * MOBA_HEAD_DIM)
            qh_ref[hd] = q16[:, sl]
            kh_ref[hd] = k16[:, sl]
            vt_ref[hd] = vt16[sl, :]
    else:
        rest[0][...] = q
        rest = rest[1:]
    u_ref, gq_ref, gk_ref, gv_ref, gr_ref, lf_ref = rest
    u_ref[...] = proj(_OFF_U, S5_WIDTH)
    gq_ref[...] = proj(_OFF_GQ, GLA_KEY_WIDTH) * (GLA_DK ** -0.5)
    gk_ref[...] = proj(_OFF_GK, GLA_KEY_WIDTH)
    gv_ref[...] = proj(_OFF_GV, GLA_VAL_WIDTH)
    gr_ref[...] = proj(_OFF_GR, GLA_VAL_WIDTH)
    glr = proj(_OFF_GLR, LANE)
    gate = _mm(glr.astype(BF16), wg_ref[...]) + bg_ref[...]
    lf_ref[...] = _log_sigmoid(gate) * (1.0 / GLA_GATE_NORM)


def _in_proj(x2d, nseq, seqlen, norm_g, w_pad, wg_pad, bg, head_major):
    n = nseq * seqlen
    tm = min(512, seqlen)
    nt = seqlen // tm
    tok = lambda width: pl.BlockSpec((tm, width), lambda b, t: (b * nt + t, 0))
    const = lambda shape: pl.BlockSpec(shape, lambda b, t: (0,) * len(shape))
    tokf = lambda width: jax.ShapeDtypeStruct((n, width), F32)
    if head_major:
        head = pl.BlockSpec((None, MOBA_HEADS, tm, MOBA_HEAD_DIM), lambda b, t: (b, 0, t, 0))
        head_t = pl.BlockSpec((None, MOBA_HEADS, MOBA_HEAD_DIM, tm), lambda b, t: (b, 0, 0, t))
        q_specs = [head, head, head_t]
        q_shapes = [jax.ShapeDtypeStruct((nseq, MOBA_HEADS, seqlen, MOBA_HEAD_DIM), BF16)] * 2 + [
            jax.ShapeDtypeStruct((nseq, MOBA_HEADS, MOBA_HEAD_DIM, seqlen), BF16)]
    else:
        q_specs = [tok(MOBA_WIDTH)]
        q_shapes = [tokf(MOBA_WIDTH)]
    return pl.pallas_call(
        functools.partial(_in_proj_kernel, head_major=head_major),
        grid=(nseq, nt),
        in_specs=[tok(D_MODEL), const((1, D_MODEL)), const((D_MODEL, IN_COLS_PAD)),
                  const((LANE, GLA_KEY_WIDTH)), const((1, GLA_KEY_WIDTH))],
        out_specs=[tok(MOBA_WIDTH), tok(MOBA_WIDTH)] + q_specs + [
            pl.BlockSpec((tm, S5_WIDTH), lambda b, t: (t, b)),
            tok(GLA_KEY_WIDTH), tok(GLA_KEY_WIDTH), tok(GLA_VAL_WIDTH), tok(GLA_VAL_WIDTH),
            tok(GLA_KEY_WIDTH)],
        out_shape=[tokf(MOBA_WIDTH), tokf(MOBA_WIDTH)] + q_shapes + [
            jax.ShapeDtypeStruct((seqlen, nseq * S5_WIDTH), F32),
            tokf(GLA_KEY_WIDTH), tokf(GLA_KEY_WIDTH), tokf(GLA_VAL_WIDTH), tokf(GLA_VAL_WIDTH),
            tokf(GLA_KEY_WIDTH)],
        compiler_params=_cparams(("parallel", "parallel")),
        name="in_proj",
    )(x2d, norm_g, w_pad, wg_pad, bg)


def _moba_prompt_kernel(q_ref, k_ref, vt_ref, o_ref, means_ref, bias_ref, m_ref, l_ref, acc_ref, alpha_ref,
                        s_ref, p_ref, *, nblk, nblk_pad):
    i = pl.program_id(2)
    bs = MOBA_BLOCK
    tile = 2 * bs

    @pl.when(i == 0)
    def _():
        means_ref[...] = jnp.zeros_like(means_ref)
        for hh in range(2):
            for j in range(nblk):
                kj = k_ref[hh, j * bs:(j + 1) * bs, :].astype(F32)
                means_ref[hh, j:j + 1, :] = jnp.mean(kj, axis=0, keepdims=True)

    blk_iota = lax.broadcasted_iota(jnp.int32, (nblk_pad, bs), 0)
    key_iota = lax.broadcasted_iota(jnp.int32, (bs, bs), 0)
    qry_iota = lax.broadcasted_iota(jnp.int32, (bs, bs), 1)
    causal_bias = jnp.where(key_iota <= qry_iota, 0.0, NEG_INF)
    own_first = (i % 2) == 0
    t_own = i // 2
    r_own = pl.multiple_of(t_own * tile, tile)

    for hh in range(2):
        qb = q_ref[hh]
        gate = _nt(means_ref[hh].astype(BF16), qb)
        gate = jnp.where(blk_iota < i, gate, NEG_INF)
        for j in range(nblk):
            gj = gate[j:j + 1, :]
            ahead = (gate > gj) | ((gate == gj) & (blk_iota < j))
            cnt = jnp.sum(ahead.astype(F32), axis=0, keepdims=True)
            past_bias = jnp.where(j < i, 0.0, NEG_INF)
            bias_ref[hh, j:j + 1, :] = jnp.where(cnt < MOBA_TOPK, past_bias, NEG_INF)

        s = _nt(k_ref[hh, pl.ds(r_own, tile), :], qb)
        prev = jnp.broadcast_to(bias_ref[hh, pl.ds(2 * t_own, 1), :], (bs, bs))
        first = jnp.where(own_first, causal_bias, prev)
        second = jnp.where(own_first, NEG_INF, causal_bias)
        s = s + jnp.concatenate([first, second], axis=0)
        m = jnp.max(s, axis=0, keepdims=True)
        p = jnp.exp(s - m)
        m_ref[hh] = m
        l_ref[hh] = jnp.sum(p, axis=0, keepdims=True)
        p_ref[1, hh] = p.astype(BF16)
        alpha_ref[hh] = jnp.zeros((1, bs), F32)
        acc_ref[hh] = jnp.zeros((MOBA_HEAD_DIM, bs), F32)

    def scores(t, slot):
        c0 = t * tile if isinstance(t, int) else pl.multiple_of(t * tile, tile)
        live = t < t_own
        for hh in range(2):
            s = _nt(k_ref[hh, pl.ds(c0, tile), :], q_ref[hh])
            r0 = jnp.where(live, bias_ref[hh, pl.ds(2 * t, 1), :], NEG_INF)
            r1 = jnp.where(live, bias_ref[hh, pl.ds(2 * t + 1, 1), :], NEG_INF)
            s_ref[slot, hh, :bs, :] = s[:bs] + r0
            s_ref[slot, hh, bs:, :] = s[bs:] + r1

    def softmax(slot):
        alphas = []
        for hh in range(2):
            s = s_ref[slot, hh]
            m = m_ref[hh]
            m_new = jnp.maximum(m, jnp.max(s, axis=0, keepdims=True))
            alpha = jnp.exp(m - m_new)
            p = jnp.exp(s - m_new)
            l_ref[hh] = alpha * l_ref[hh] + jnp.sum(p, axis=0, keepdims=True)
            m_ref[hh] = m_new
            p_ref[slot, hh] = p.astype(BF16)
            alphas.append(alpha)
        return alphas

    def values(slot, v0, alphas):
        for hh in range(2):
            acc_ref[hh] = alphas[hh] * acc_ref[hh] + _mm(vt_ref[hh, :, pl.ds(v0, tile)], p_ref[slot, hh])

    scores(0, 0)
    n_pair = (t_own + 1) // 2
    last_tile = nblk // 2 - 1

    def body(u, carry):
        pending = jnp.where(u == 0, r_own, (2 * u - 1) * tile)
        values(1, pl.multiple_of(pending, tile), [alpha_ref[0], alpha_ref[1]])
        scores(2 * u + 1, 1)
        alphas = softmax(0)
        values(0, pl.multiple_of(2 * u * tile, tile), alphas)
        scores(jnp.minimum(2 * u + 2, last_tile), 0)
        alphas = softmax(1)
        alpha_ref[0] = alphas[0]
        alpha_ref[1] = alphas[1]
        return carry

    lax.fori_loop(0, n_pair, body, 0)
    pending = jnp.where(n_pair == 0, r_own, (2 * n_pair - 1) * tile)
    values(1, pl.multiple_of(pending, tile), [alpha_ref[0], alpha_ref[1]])
    out = jnp.concatenate([acc_ref[0] / l_ref[0], acc_ref[1] / l_ref[1]], axis=0)
    o_ref[...] = out.T


def _moba_prompt(qh, kh, vt):
    nseq, _, seqlen, dh = qh.shape
    bs = MOBA_BLOCK
    assert seqlen % (2 * bs) == 0
    nblk = seqlen // bs
    nblk_pad = -(-nblk // 8) * 8
    kern = functools.partial(_moba_prompt_kernel, nblk=nblk, nblk_pad=nblk_pad)
    return pl.pallas_call(
        kern,
        grid=(nseq, MOBA_HEADS // 2, nblk),
        in_specs=[pl.BlockSpec((None, 2, bs, dh), lambda b, h, i: (b, h, i, 0)),
                  pl.BlockSpec((None, 2, seqlen, dh), lambda b, h, i: (b, h, 0, 0)),
                  pl.BlockSpec((None, 2, dh, seqlen), lambda b, h, i: (b, h, 0, 0))],
        out_specs=pl.BlockSpec((bs, 2 * dh), lambda b, h, i: (b * nblk + i, h)),
        out_shape=jax.ShapeDtypeStruct((nseq * seqlen, MOBA_WIDTH), F32),
        scratch_shapes=[pltpu.VMEM((2, nblk_pad, dh), F32), pltpu.VMEM((2, nblk_pad, bs), F32),
                        pltpu.VMEM((2, 1, bs), F32), pltpu.VMEM((2, 1, bs), F32),
                        pltpu.VMEM((2, dh, bs), F32), pltpu.VMEM((2, 1, bs), F32),
                        pltpu.VMEM((2, 2, 2 * bs, bs), F32), pltpu.VMEM((2, 2, 2 * bs, bs), BF16)],
        compiler_params=_cparams(("parallel", "parallel", "arbitrary")),
        name="moba_prompt",
    )(qh, kh, vt)


def _moba_gate_kernel(pt_ref, qbd_ref, *rest, ppg):
    del pt_ref
    k_refs = rest[:ppg]
    sel_ref, g_ref = rest[ppg:]
    s_id = pl.program_id(1)
    lane_blk = lax.broadcasted_iota(jnp.int32, (MOBA_HEADS, LANE), 1)

    @pl.when(s_id == 0)
    def _():
        g_ref[...] = jnp.full_like(g_ref, NEG_INF)

    qbd = qbd_ref[...].astype(BF16)
    g = g_ref[...]
    for bi in range(ppg // 2):
        s = jnp.zeros((MOBA_HEADS, PAGE_SIZE), F32)
        for pg in range(2):
            kt = k_refs[2 * bi + pg][...].reshape(MOBA_WIDTH, PAGE_SIZE)
            s = s + _mm(qbd, kt.astype(BF16))
        gate = jnp.sum(s, axis=1, keepdims=True)
        g = jnp.where(lane_blk == s_id * (ppg // 2) + bi, gate, g)
    g_ref[...] = g

    @pl.when(s_id == pl.num_programs(1) - 1)
    def _():
        gg = g_ref[...]
        lane_f = lane_blk.astype(F32)
        picked = jnp.zeros((MOBA_HEADS, LANE), F32)
        for r in range(MOBA_TOPK):
            mx = jnp.max(gg, axis=1, keepdims=True)
            idx = jnp.min(jnp.where(gg == mx, lane_f, float(LANE)), axis=1, keepdims=True)
            picked = jnp.where(lane_blk == r, idx, picked)
            gg = jnp.where(lane_f == idx, NEG_INF, gg)
        sel_ref[...] = picked.astype(jnp.int32)


def _moba_pick_kernel(pt_ref, sel_ref, q_ref, kn_ref, vn_ref, *rest):
    del pt_ref, sel_ref
    npg = 2 * MOBA_TOPK
    k_refs = rest[:npg]
    v_refs = rest[npg:2 * npg]
    o_ref = rest[2 * npg]
    scale = MOBA_HEAD_DIM ** -0.5
    sub = 8

    q = q_ref[...]
    q8 = jnp.broadcast_to((q * scale).astype(BF16), (sub, MOBA_HEAD_DIM))
    s_own = jnp.sum(q * kn_ref[...], axis=1, keepdims=True) * scale
    s = [_mm(q8, k_refs[j][...].astype(BF16))[0:1, :] for j in range(npg)]
    m = s_own
    for sj in s:
        m = jnp.maximum(m, jnp.max(sj, axis=1, keepdims=True))
    p_own = jnp.exp(s_own - m)
    den = p_own
    num = p_own * vn_ref[...]
    for j in range(npg):
        p = jnp.exp(s[j] - m)
        den = den + jnp.sum(p, axis=1, keepdims=True)
        p8 = jnp.broadcast_to(p.astype(BF16), (sub, PAGE_SIZE))
        num = num + _nt(p8, v_refs[j][...].astype(BF16))[0:1, :]
    o_ref[...] = num / den


def _moba_decode(page_table, q, k_new, v_new, cache_kt, cache_vt, layer):
    nseq, n_pages = page_table.shape
    nblk = n_pages * PAGE_SIZE // MOBA_BLOCK
    assert n_pages * PAGE_SIZE == nblk * MOBA_BLOCK and MOBA_TOPK <= nblk <= LANE
    ppg = 16 if n_pages % 16 == 0 else 2
    heads, dh = MOBA_HEADS, MOBA_HEAD_DIM

    qbd = (q[:, :, None, :] * jnp.eye(heads, dtype=F32)[None, :, :, None]).reshape(nseq, heads, heads * dh)

    def all_heads(p):
        return pl.BlockSpec((None, None, heads, dh, PAGE_SIZE),
                            lambda b, s, pt, p=p: (pt[b, s * ppg + p], layer, 0, 0, 0))

    sel = pl.pallas_call(
        functools.partial(_moba_gate_kernel, ppg=ppg),
        grid_spec=pltpu.PrefetchScalarGridSpec(
            num_scalar_prefetch=1,
            grid=(nseq, n_pages // ppg),
            in_specs=[pl.BlockSpec((None, heads, heads * dh), lambda b, s, pt: (b, 0, 0))]
            + [all_heads(p) for p in range(ppg)],
            out_specs=pl.BlockSpec((None, heads, LANE), lambda b, s, pt: (b, 0, 0)),
            scratch_shapes=[pltpu.VMEM((heads, LANE), F32)],
        ),
        out_shape=jax.ShapeDtypeStruct((nseq, heads, LANE), jnp.int32),
        compiler_params=_cparams(("parallel", "arbitrary")),
        name="moba_gate",
    )(page_table, qbd, *([cache_kt] * ppg))
    sel_flat = sel[:, :, :MOBA_TOPK].reshape(nseq * heads * MOBA_TOPK)

    def one_head(j):
        def index(b, h, pt, sl):
            blk = sl[(b * heads + h) * MOBA_TOPK + j // 2]
            return (pt[b, 2 * blk + j % 2], layer, h, 0, 0)
        return pl.BlockSpec((None, None, None, dh, PAGE_SIZE), index)

    row = pl.BlockSpec((None, None, 1, dh), lambda b, h, pt, sl: (b, h, 0, 0))
    rows = lambda a: a.reshape(nseq, heads, 1, dh)
    pages = [one_head(j) for j in range(2 * MOBA_TOPK)]
    out = pl.pallas_call(
        _moba_pick_kernel,
        grid_spec=pltpu.PrefetchScalarGridSpec(
            num_scalar_prefetch=2,
            grid=(nseq, heads),
            in_specs=[row, row, row] + pages + pages,
            out_specs=row,
        ),
        out_shape=jax.ShapeDtypeStruct((nseq, heads, 1, dh), F32),
        compiler_params=_cparams(("parallel", "parallel")),
        name="moba_pick",
    )(page_table, sel_flat, rows(q), rows(k_new), rows(v_new),
      *([cache_kt] * (2 * MOBA_TOPK)), *([cache_vt] * (2 * MOBA_TOPK)))
    return out.reshape(nseq, heads * dh)


def _s5_disc_kernel(are_ref, aim_ref, ls_ref, bre_ref, bim_ref, abr_ref, abi_ref, bbr_ref, bbi_ref):
    a_re = are_ref[...]
    a_im = aim_ref[...]
    step = jnp.exp(ls_ref[...])
    mag = jnp.exp(a_re * step)
    ab_re = mag * jnp.cos(a_im * step)
    ab_im = mag * jnp.sin(a_im * step)
    den = a_re * a_re + a_im * a_im
    n_re = ab_re - 1.0
    f_re = (n_re * a_re + ab_im * a_im) / den
    f_im = (ab_im * a_re - n_re * a_im) / den
    b_re = bre_ref[...]
    b_im = bim_ref[...]
    abr_ref[...] = ab_re
    abi_ref[...] = ab_im
    bbr_ref[...] = f_re * b_re - f_im * b_im
    bbi_ref[...] = f_re * b_im + f_im * b_re


def _s5_discretise(a_re, a_im, log_step, b_re, b_im):
    depth = a_re.shape[0]
    rows = depth * S5_GROUPS * S5_GROUP
    rep = lambda a: jnp.broadcast_to(a[:, :, None, :], (depth, S5_GROUPS, S5_GROUP, S5_STATE)).reshape(rows, S5_STATE)
    ls = jnp.broadcast_to(log_step[:, :, None, None], (depth, S5_GROUPS, S5_GROUP, S5_STATE)).reshape(rows, S5_STATE)
    bt = lambda b: jnp.swapaxes(b, 2, 3).reshape(rows, S5_STATE)
    shp = jax.ShapeDtypeStruct((rows, S5_STATE), F32)
    abr, abi, bbr, bbi = pl.pallas_call(
        _s5_disc_kernel, out_shape=[shp] * 4, name="s5_disc",
    )(rep(a_re), rep(a_im), ls, bt(b_re), bt(b_im))
    r4 = lambda a: a.reshape(depth, S5_GROUPS, S5_GROUP, S5_STATE)
    ab = jnp.concatenate([r4(abr)[:, :, 0].reshape(depth, 1, S5_CH),
                          r4(abi)[:, :, 0].reshape(depth, 1, S5_CH)], axis=1)
    eye = jnp.eye(S5_GROUPS, dtype=F32)
    bd = lambda a: (r4(a)[:, :, :, None, :] * eye[None, :, None, :, None]).reshape(depth, S5_WIDTH, S5_CH)
    bb = jnp.concatenate([bd(bbr), bd(bbi)], axis=2)
    return ab, bb.astype(BF16)


def _s5_out_map(c_re, c_im):
    depth = c_re.shape[0]
    eye = jnp.eye(S5_GROUPS, dtype=F32)
    bd = lambda c: (jnp.swapaxes(c, 2, 3)[:, :, :, None, :] * eye[None, :, None, :, None]).reshape(depth, S5_CH, S5_WIDTH)
    return jnp.concatenate([bd(c_re), -bd(c_im)], axis=1).astype(BF16)


S5_SCAN_LANES = 512


def _s5_kernel(u_ref, bb_ref, ab_ref, h0_ref, cc_ref, d_ref, wglu_ref, bglu_ref,
               o_ref, hfin_ref, xs_ref, hc_ref, *, nb, tt):
    @pl.when(pl.program_id(0) == 0)
    def _():
        hc_ref[...] = h0_ref[...]

    u = u_ref[...]
    xs_ref[...] = _mm(u.astype(BF16), bb_ref[...])

    for c0 in range(0, S5_CH, S5_SCAN_LANES):
        re_sl = slice(c0, c0 + S5_SCAN_LANES)
        im_sl = slice(S5_CH + c0, S5_CH + c0 + S5_SCAN_LANES)
        ar = jnp.broadcast_to(ab_ref[0:1, re_sl], (nb, S5_SCAN_LANES))
        ai = jnp.broadcast_to(ab_ref[1:2, re_sl], (nb, S5_SCAN_LANES))

        def step(t, carry, re_sl=re_sl, im_sl=im_sl, ar=ar, ai=ai):
            hr, hi = carry
            rows = pl.ds(pl.multiple_of(t * nb, nb), nb)
            nhr = ar * hr - ai * hi + xs_ref[rows, re_sl]
            nhi = ar * hi + ai * hr + xs_ref[rows, im_sl]
            xs_ref[rows, re_sl] = nhr
            xs_ref[rows, im_sl] = nhi
            return nhr, nhi

        hr, hi = lax.fori_loop(0, tt, step, (hc_ref[:, re_sl], hc_ref[:, im_sl]), unroll=min(4, tt))
        hc_ref[:, re_sl] = hr
        hc_ref[:, im_sl] = hi

    y = _mm(xs_ref[...].astype(BF16), cc_ref[...]) + d_ref[...] * u
    yg = _gelu_tanh(y)
    gate = _sigmoid(_mm(yg.astype(BF16), wglu_ref[...]) + bglu_ref[...])
    o_ref[...] = yg * gate
    hfin_ref[...] = hc_ref[...]


def _s5(u_tm, nb, bb, ab, h0, cc, d, wglu, bglu):
    rows = u_tm.shape[0]
    t_total = rows // nb
    tt = min(64, t_total)
    const = lambda shape: pl.BlockSpec(shape, lambda t: (0,) * len(shape))
    kern = functools.partial(_s5_kernel, nb=nb, tt=tt)
    return pl.pallas_call(
        kern,
        grid=(t_total // tt,),
        in_specs=[pl.BlockSpec((tt * nb, S5_WIDTH), lambda t: (t, 0)),
                  const((S5_WIDTH, 2 * S5_CH)), const((2, S5_CH)), const((nb, 2 * S5_CH)),
                  const((2 * S5_CH, S5_WIDTH)), const((1, S5_WIDTH)),
                  const((S5_WIDTH, S5_WIDTH)), const((1, S5_WIDTH))],
        out_specs=[pl.BlockSpec((tt * nb, S5_WIDTH), lambda t: (t, 0)), const((nb, 2 * S5_CH))],
        out_shape=[jax.ShapeDtypeStruct((rows, S5_WIDTH), F32), jax.ShapeDtypeStruct((nb, 2 * S5_CH), F32)],
        scratch_shapes=[pltpu.VMEM((tt * nb, 2 * S5_CH), F32), pltpu.VMEM((nb, 2 * S5_CH), F32)],
        compiler_params=_cparams(("arbitrary",)),
        name="s5_scan",
    )(u_tm, bb, ab, h0, cc, d, wglu, bglu)


def _split3(x):
    hi = x.astype(BF16)
    r = x - hi.astype(F32)
    mid = r.astype(BF16)
    lo = (r - mid.astype(F32)).astype(BF16)
    return hi, mid, lo


def _gla_prompt_kernel(gq_ref, gk_ref, gv_ref, gr_ref, lf_ref, gn_ref, o_ref, sfin_ref, st_ref, *, nchunk):
    c = GLA_CHUNK

    @pl.when(pl.program_id(1) == 0)
    def _():
        st_ref[...] = jnp.zeros_like(st_ref)

    row = lax.broadcasted_iota(jnp.int32, (c, c), 0)
    col = lax.broadcasted_iota(jnp.int32, (c, c), 1)
    causal = col <= row
    tri = causal.astype(BF16)
    gn = gn_ref[...]

    for ci in range(nchunk):
        rows = slice(ci * c, (ci + 1) * c)
        g_hi, g_mid, g_lo = _split3(lf_ref[rows, :])
        bcum = _mm(tri, g_hi) + _mm(tri, g_mid) + _mm(tri, g_lo)
        blast = bcum[c - 1:c, :]
        qd = (gq_ref[rows, :] * jnp.exp(bcum)).astype(BF16)
        kd = (gk_ref[rows, :] * jnp.exp(-bcum)).astype(BF16)
        kt = (gk_ref[rows, :] * jnp.exp(blast - bcum)).astype(BF16)
        st = st_ref[...]
        st_ref[...] = st * jnp.exp(blast)
        st16 = st.astype(BF16)
        for hd in range(GLA_HEADS):
            ks = slice(hd * GLA_DK, (hd + 1) * GLA_DK)
            vs = slice(hd * GLA_DV, (hd + 1) * GLA_DV)
            vh = gv_ref[rows, vs].astype(BF16)
            att = jnp.where(causal, _nt(qd[:, ks], kd[:, ks]), 0.0)
            o = _nt(qd[:, ks], st16[:, ks]) + _mm(att.astype(BF16), vh)
            st_ref[:, ks] += _tn(vh, kt[:, ks])
            o_ref[rows, vs] = _rms(o, gn) * _silu(gr_ref[rows, vs])

    sfin_ref[...] = st_ref[...]


def _gla_prompt(gq, gk, gv, gr, lf, gnorm, nseq, seqlen):
    assert seqlen % GLA_CHUNK == 0
    tt = min(256, seqlen)
    nt = seqlen // tt
    tok = lambda width: pl.BlockSpec((tt, width), lambda b, t: (b * nt + t, 0))
    kern = functools.partial(_gla_prompt_kernel, nchunk=tt // GLA_CHUNK)
    return pl.pallas_call(
        kern,
        grid=(nseq, nt),
        in_specs=[tok(GLA_KEY_WIDTH), tok(GLA_KEY_WIDTH), tok(GLA_VAL_WIDTH), tok(GLA_VAL_WIDTH),
                  tok(GLA_KEY_WIDTH), pl.BlockSpec((1, GLA_DV), lambda b, t: (0, 0))],
        out_specs=[tok(GLA_VAL_WIDTH),
                   pl.BlockSpec((None, GLA_DV, GLA_KEY_WIDTH), lambda b, t: (b, 0, 0))],
        out_shape=[jax.ShapeDtypeStruct((nseq * seqlen, GLA_VAL_WIDTH), F32),
                   jax.ShapeDtypeStruct((nseq, GLA_DV, GLA_KEY_WIDTH), F32)],
        scratch_shapes=[pltpu.VMEM((GLA_DV, GLA_KEY_WIDTH), F32)],
        compiler_params=_cparams(("parallel", "arbitrary")),
        name="gla_prompt",
    )(gq, gk, gv, gr, lf, gnorm)


def _gla_decode_kernel(q_ref, k_ref, g_ref, v_ref, r_ref, s0_ref, gn_ref, o_ref, s_ref):
    s = jnp.exp(g_ref[...]) * s0_ref[...] + k_ref[...] * v_ref[...]
    s_ref[...] = s
    o = jnp.sum(q_ref[...] * s, axis=1, keepdims=True)
    o_ref[...] = _rms(o, gn_ref[...]) * _silu(r_ref[...])


def _gla_decode(gq, gk, lf, gv, gr, s0, gnorm):
    n = gq.shape[0]
    col = lambda a: a.reshape(n, GLA_HEADS, GLA_DK, 1)
    rowv = lambda a: a.reshape(n, GLA_HEADS, 1, GLA_DV)
    cspec = pl.BlockSpec((None, GLA_HEADS, GLA_DK, 1), lambda b: (b, 0, 0, 0))
    rspec = pl.BlockSpec((None, GLA_HEADS, 1, GLA_DV), lambda b: (b, 0, 0, 0))
    sspec = pl.BlockSpec((None, GLA_HEADS, GLA_DK, GLA_DV), lambda b: (b, 0, 0, 0))
    o, s = pl.pallas_call(
        _gla_decode_kernel,
        grid=(n,),
        in_specs=[cspec, cspec, cspec, rspec, rspec, sspec,
                  pl.BlockSpec((1, 1, GLA_DV), lambda b: (0, 0, 0))],
        out_specs=[rspec, sspec],
        out_shape=[jax.ShapeDtypeStruct((n, GLA_HEADS, 1, GLA_DV), F32),
                   jax.ShapeDtypeStruct((n, GLA_HEADS, GLA_DK, GLA_DV), F32)],
        compiler_params=_cparams(("parallel",)),
        name="gla_decode",
    )(col(gq), col(gk), col(lf), rowv(gv), rowv(gr), s0, gnorm.reshape(1, 1, GLA_DV))
    return o.reshape(n, GLA_VAL_WIDTH), s


def _out_ffn_kernel(x_ref, a_ref, b_ref, c_ref, wo_ref, nf_ref, wg_ref, wu_ref, wd_ref, nfin_ref,
                    o_ref, hf_ref, acc_ref, *, final_norm):
    j = pl.program_id(2)

    @pl.when(j == 0)
    def _():
        mix = jnp.concatenate([a_ref[...], b_ref[...], c_ref[...]], axis=1).astype(BF16)
        x1 = x_ref[...] + _mm(mix, wo_ref[...])
        acc_ref[...] = x1
        hf_ref[...] = _rms(x1, nf_ref[...]).astype(BF16)

    hf = hf_ref[...]
    act = _silu(_mm(hf, wg_ref[...])) * _mm(hf, wu_ref[...])
    acc_ref[...] += _mm(act.astype(BF16), wd_ref[...])

    @pl.when(j == pl.num_programs(2) - 1)
    def _():
        y = acc_ref[...]
        o_ref[...] = _rms(y, nfin_ref[...]) if final_norm else y


def _out_ffn(x2d, a, b_tm, c, nseq, seqlen, w_out, norm_ffn, w_gate, w_up, w_down, norm_final, final_norm):
    tm = min(512, seqlen)
    nt = seqlen // tm
    th = 256
    nh = FFN_HIDDEN // th
    tok = lambda width: pl.BlockSpec((tm, width), lambda b, t, j: (b * nt + t, 0))
    const = lambda shape: pl.BlockSpec(shape, lambda b, t, j: (0,) * len(shape))
    kern = functools.partial(_out_ffn_kernel, final_norm=final_norm)
    return pl.pallas_call(
        kern,
        grid=(nseq, nt, nh),
        in_specs=[tok(D_MODEL), tok(MOBA_WIDTH),
                  pl.BlockSpec((tm, S5_WIDTH), lambda b, t, j: (t, b)),
                  tok(GLA_VAL_WIDTH), const((D_MODEL, D_MODEL)), const((1, D_MODEL)),
                  pl.BlockSpec((D_MODEL, th), lambda b, t, j: (0, j)),
                  pl.BlockSpec((D_MODEL, th), lambda b, t, j: (0, j)),
                  pl.BlockSpec((th, D_MODEL), lambda b, t, j: (j, 0)),
                  const((1, D_MODEL))],
        out_specs=tok(D_MODEL),
        out_shape=jax.ShapeDtypeStruct((nseq * seqlen, D_MODEL), F32),
        scratch_shapes=[pltpu.VMEM((tm, D_MODEL), BF16), pltpu.VMEM((tm, D_MODEL), F32)],
        compiler_params=_cparams(("parallel", "parallel", "arbitrary")),
        name="out_ffn",
    )(x2d, a, b_tm, c, w_out, norm_ffn, w_gate, w_up, w_down, norm_final)


def _prep_weights(w_in, w_out, s5_w_glu, gla_w_gate, w_ffn_gate, w_ffn_up, w_ffn_down):
    depth = w_in.shape[0]
    w_in_p = jnp.pad(w_in, ((0, 0), (0, 0), (0, IN_COLS_PAD - IN_COLS))).astype(BF16)
    wg_p = jnp.pad(gla_w_gate, ((0, 0), (0, LANE - GLA_GATE_RANK), (0, 0))).astype(BF16)
    del depth
    return (w_in_p, w_out.astype(BF16), s5_w_glu.astype(BF16), wg_p,
            w_ffn_gate.astype(BF16), w_ffn_up.astype(BF16), w_ffn_down.astype(BF16))


def _trunk(x, kv_pool, ssm_re0, ssm_im0, gla0, params):
    (norm_mix, w_in_p, w_out, ab, bb, cc, s5_d, w_glu, s5_b_glu, wg_p, gla_b_gate, gla_norm,
     norm_ffn, w_gate, w_up, w_down, norm_final) = params
    nseq, seqlen, _ = x.shape
    depth = w_in_p.shape[0]
    decode = kv_pool is not None
    if decode:
        assert seqlen == 1
        cache_k, cache_v, page_table = kv_pool
        cache_kt = jnp.transpose(cache_k, (0, 1, 3, 4, 2))
        cache_vt = jnp.transpose(cache_v, (0, 1, 3, 4, 2))
        tw_nseq, tw_len = 1, nseq
    else:
        assert seqlen % MOBA_BLOCK == 0
        tw_nseq, tw_len = nseq, seqlen
    n = nseq * seqlen
    x2d = x.reshape(n, D_MODEL)
    k_rows, v_rows, s_re, s_im, s_gla = [], [], [], [], []
    for l in range(depth):
        k, v, *q_parts, u_tm, gq, gk, gv, gr, lf = _in_proj(
            x2d, tw_nseq, tw_len, norm_mix[l][None], w_in_p[l], wg_p[l], gla_b_gate[l][None],
            head_major=not decode)
        k_rows.append(k)
        v_rows.append(v)
        if decode:
            heads = lambda a: a.reshape(nseq, MOBA_HEADS, MOBA_HEAD_DIM)
            a_out = _moba_decode(page_table, heads(q_parts[0]), heads(k), heads(v), cache_kt, cache_vt, l)
        else:
            a_out = _moba_prompt(*q_parts)
        if ssm_re0 is None:
            h0 = jnp.zeros((nseq, 2 * S5_CH), F32)
        else:
            h0 = jnp.concatenate([ssm_re0[l].reshape(nseq, S5_CH), ssm_im0[l].reshape(nseq, S5_CH)], axis=1)
        b_tm, h_fin = _s5(u_tm.reshape(n, S5_WIDTH), nseq, bb[l], ab[l], h0, cc[l], s5_d[l][None],
                          w_glu[l], s5_b_glu[l][None])
        s_re.append(h_fin[:, :S5_CH].reshape(nseq, S5_GROUPS, S5_STATE))
        s_im.append(h_fin[:, S5_CH:].reshape(nseq, S5_GROUPS, S5_STATE))
        b_tm = b_tm.reshape(tw_len, tw_nseq * S5_WIDTH)
        if decode:
            c_out, s_fin = _gla_decode(gq, gk, lf, gv, gr, gla0[l], gla_norm[l])
        else:
            c_out, s_t = _gla_prompt(gq, gk, gv, gr, lf, gla_norm[l][None], nseq, seqlen)
            s_fin = jnp.transpose(s_t.reshape(nseq, GLA_DV, GLA_HEADS, GLA_DK), (0, 2, 3, 1))
        s_gla.append(s_fin)
        x2d = _out_ffn(x2d, a_out, b_tm, c_out, tw_nseq, tw_len, w_out[l], norm_ffn[l][None],
                       w_gate[l], w_up[l], w_down[l], norm_final[None], final_norm=(l == depth - 1))
    shp = (nseq, seqlen, MOBA_HEADS, MOBA_HEAD_DIM)
    return (x2d.reshape(nseq, seqlen, D_MODEL),
            jnp.stack([k.reshape(shp) for k in k_rows], axis=1),
            jnp.stack([v.reshape(shp) for v in v_rows], axis=1),
            jnp.stack(s_re, axis=0), jnp.stack(s_im, axis=0), jnp.stack(s_gla, axis=0))


def kernel(x_prompt, x_sample, cache_k, cache_v, page_table, state_ssm_re, state_ssm_im, state_gla,
           norm_mix, w_in, w_out, s5_a_re, s5_a_im, s5_log_step, s5_b_re, s5_b_im, s5_c_re, s5_c_im,
           s5_d, s5_w_glu, s5_b_glu, gla_w_gate, gla_b_gate, gla_norm, norm_ffn, w_ffn_gate, w_ffn_up,
           w_ffn_down, norm_final):
    w_in_p, w_out16, w_glu16, wg_p, w_gate16, w_up16, w_down16 = _prep_weights(
        w_in, w_out, s5_w_glu, gla_w_gate, w_ffn_gate, w_ffn_up, w_ffn_down)
    ab, bb = _s5_discretise(s5_a_re, s5_a_im, s5_log_step, s5_b_re, s5_b_im)
    cc = _s5_out_map(s5_c_re, s5_c_im)
    params = (norm_mix, w_in_p, w_out16, ab, bb, cc, s5_d, w_glu16, s5_b_glu, wg_p, gla_b_gate, gla_norm,
              norm_ffn, w_gate16, w_up16, w_down16, norm_final)
    y_p, k_p, v_p, re_p, im_p, gla_p = _trunk(x_prompt, None, None, None, None, params)
    y_s, k_s, v_s, re_s, im_s, gla_s = _trunk(x_sample, (cache_k, cache_v, page_table),
                                              state_ssm_re, state_ssm_im, state_gla, params)
    return (y_p, y_s, k_p, v_p, k_s, v_s, re_p, im_p, re_s, im_s, gla_p, gla_s)
```

```python
import functools
import math

import jax
import jax.numpy as jnp
from jax import lax
from jax.experimental import pallas as pl
from jax.experimental.pallas import tpu as pltpu

F32 = jnp.float32
BF16 = jnp.bfloat16

D_MODEL = 1024
PAGE_SIZE = 128
MOBA_HEAD_DIM = 64
MOBA_HEADS = 8
MOBA_WIDTH = MOBA_HEADS * MOBA_HEAD_DIM
MOBA_BLOCK = 256
MOBA_TOPK = 3
MOBA_VROWS = 80
LOG2E = 1.4426950408889634
S5_WIDTH = 256
S5_GROUP = 16
S5_GROUPS = 16
S5_STATE = 64
S5_CH = S5_GROUPS * S5_STATE
GLA_HEADS = 4
GLA_DK = 32
GLA_DV = 64
GLA_KEY_WIDTH = GLA_HEADS * GLA_DK
GLA_VAL_WIDTH = GLA_HEADS * GLA_DV
GLA_GATE_RANK = 16
GLA_GATE_NORM = 16.0
GLA_CHUNK = 64
FFN_HIDDEN = 2816
RMS_EPS = 1e-6
NEG_INF = float("-inf")

_OFF_Q = 0
_OFF_K = _OFF_Q + MOBA_WIDTH
_OFF_V = _OFF_K + MOBA_WIDTH
_OFF_U = _OFF_V + MOBA_WIDTH
_OFF_GQ = _OFF_U + S5_WIDTH
_OFF_GK = _OFF_GQ + GLA_KEY_WIDTH
_OFF_GV = _OFF_GK + GLA_KEY_WIDTH
_OFF_GR = _OFF_GV + GLA_VAL_WIDTH
_OFF_GLR = _OFF_GR + GLA_VAL_WIDTH
IN_COLS = _OFF_GLR + GLA_GATE_RANK
LANE = 128
IN_COLS_PAD = _OFF_GLR + LANE

VMEM_LIMIT = 56 * 1024 * 1024


def _cparams(sem):
    return pltpu.CompilerParams(dimension_semantics=sem, vmem_limit_bytes=VMEM_LIMIT)


def _nt(a, b):
    return lax.dot_general(a, b, (((1,), (1,)), ((), ())), preferred_element_type=F32)


def _tn(a, b):
    return lax.dot_general(a, b, (((0,), (0,)), ((), ())), preferred_element_type=F32)


def _mm(a, b):
    return jnp.dot(a, b, preferred_element_type=F32)


def _sigmoid(x):
    return 1.0 / (1.0 + jnp.exp(-x))


def _silu(x):
    return x * _sigmoid(x)


def _gelu_tanh(x):
    c = math.sqrt(2.0 / math.pi)
    return 0.5 * x * (1.0 + jnp.tanh(c * (x + 0.044715 * (x * x * x))))


def _log_sigmoid(x):
    return jnp.minimum(x, 0.0) - jnp.log(1.0 + jnp.exp(-jnp.abs(x)))


def _rms(x, g):
    return x * lax.rsqrt(jnp.mean(x * x, axis=-1, keepdims=True) + RMS_EPS) * g


def _in_proj_kernel(x_ref, g_ref, w_ref, wg_ref, bg_ref, k_ref, v_ref, *rest, head_major):
    h = _rms(x_ref[...], g_ref[...]).astype(BF16)

    def proj(lo, width):
        return _mm(h, w_ref[:, lo:lo + width])

    q = proj(_OFF_Q, MOBA_WIDTH)
    k = proj(_OFF_K, MOBA_WIDTH)
    v = proj(_OFF_V, MOBA_WIDTH)
    if head_major:
        vt = v.T
        k_ref[...] = k.T
        v_ref[...] = vt
        qh_ref, kh_ref, vt_ref = rest[:3]
        rest = rest[3:]
        q16 = (q * (MOBA_HEAD_DIM ** -0.5 * LOG2E)).astype(BF16)
        k16 = k.astype(BF16)
        vt16 = vt.astype(BF16)
        for hd in range(MOBA_HEADS):
            sl = slice(hd * MOBA_HEAD_DIM, (hd + 1) * MOBA_HEAD_DIM)
            qh_ref[hd] = q16[:, sl]
            kh_ref[hd] = k16[:, sl]
            vt_ref[hd, :MOBA_HEAD_DIM, :] = vt16[sl, :]
            vt_ref[hd, MOBA_HEAD_DIM:, :] = jnp.ones((MOBA_VROWS - MOBA_HEAD_DIM, vt16.shape[1]), BF16)
    else:
        k_ref[...] = k
        v_ref[...] = v
        rest[0][...] = q
        rest = rest[1:]
    u_ref, gq_ref, gk_ref, gv_ref, gr_ref, lf_ref = rest
    u_ref[...] = proj(_OFF_U, S5_WIDTH)
    gq_ref[...] = proj(_OFF_GQ, GLA_KEY_WIDTH) * (GLA_DK ** -0.5)
    gk_ref[...] = proj(_OFF_GK, GLA_KEY_WIDTH)
    gv_ref[...] = proj(_OFF_GV, GLA_VAL_WIDTH)
    gr_ref[...] = proj(_OFF_GR, GLA_VAL_WIDTH)
    glr = proj(_OFF_GLR, LANE)
    gate = _mm(glr.astype(BF16), wg_ref[...]) + bg_ref[...]
    lf_ref[...] = _log_sigmoid(gate) * (1.0 / GLA_GATE_NORM)


def _in_proj(x2d, nseq, seqlen, norm_g, w_pad, wg_pad, bg, head_major):
    n = nseq * seqlen
    tm = min(512, seqlen)
    nt = seqlen // tm
    tok = lambda width: pl.BlockSpec((tm, width), lambda b, t: (b * nt + t, 0))
    const = lambda shape: pl.BlockSpec(shape, lambda b, t: (0,) * len(shape))
    tokf = lambda width: jax.ShapeDtypeStruct((n, width), F32)
    if head_major:
        head = pl.BlockSpec((None, MOBA_HEADS, tm, MOBA_HEAD_DIM), lambda b, t: (b, 0, t, 0))
        head_t = pl.BlockSpec((None, MOBA_HEADS, MOBA_VROWS, tm), lambda b, t: (b, 0, 0, t))
        kv_t = pl.BlockSpec((None, MOBA_WIDTH, tm), lambda b, t: (b, 0, t))
        q_specs = [kv_t, kv_t, head, head, head_t]
        q_shapes = [jax.ShapeDtypeStruct((nseq, MOBA_WIDTH, seqlen), F32)] * 2 + [
            jax.ShapeDtypeStruct((nseq, MOBA_HEADS, seqlen, MOBA_HEAD_DIM), BF16)] * 2 + [
            jax.ShapeDtypeStruct((nseq, MOBA_HEADS, MOBA_VROWS, seqlen), BF16)]
    else:
        q_specs = [tok(MOBA_WIDTH), tok(MOBA_WIDTH), tok(MOBA_WIDTH)]
        q_shapes = [tokf(MOBA_WIDTH)] * 3
    return pl.pallas_call(
        functools.partial(_in_proj_kernel, head_major=head_major),
        grid=(nseq, nt),
        in_specs=[tok(D_MODEL), const((1, D_MODEL)), const((D_MODEL, IN_COLS_PAD)),
                  const((LANE, GLA_KEY_WIDTH)), const((1, GLA_KEY_WIDTH))],
        out_specs=q_specs + [
            pl.BlockSpec((tm, S5_WIDTH), lambda b, t: (t, b)),
            tok(GLA_KEY_WIDTH), tok(GLA_KEY_WIDTH), tok(GLA_VAL_WIDTH), tok(GLA_VAL_WIDTH),
            tok(GLA_KEY_WIDTH)],
        out_shape=q_shapes + [
            jax.ShapeDtypeStruct((seqlen, nseq * S5_WIDTH), F32),
            tokf(GLA_KEY_WIDTH), tokf(GLA_KEY_WIDTH), tokf(GLA_VAL_WIDTH), tokf(GLA_VAL_WIDTH),
            tokf(GLA_KEY_WIDTH)],
        compiler_params=_cparams(("parallel", "parallel")),
        name="in_proj",
    )(x2d, norm_g, w_pad, wg_pad, bg)


def _moba_prompt_kernel(q_ref, k_ref, vt_ref, o_ref, means_ref, bias_ref, m_ref, acc_ref, alpha_ref,
                        s_ref, p_ref, *, nblk, nblk_pad):
    i = pl.program_id(2)
    bs = MOBA_BLOCK
    tile = 2 * bs

    @pl.when(i == 0)
    def _():
        means_ref[...] = jnp.zeros_like(means_ref)
        for hh in range(2):
            for j in range(nblk):
                kj = k_ref[hh, j * bs:(j + 1) * bs, :].astype(F32)
                means_ref[hh, j:j + 1, :] = jnp.mean(kj, axis=0, keepdims=True)

    blk_iota = lax.broadcasted_iota(jnp.int32, (nblk_pad, bs), 0)
    blk_f = blk_iota.astype(F32)
    key_iota = lax.broadcasted_iota(jnp.int32, (bs, bs), 0)
    qry_iota = lax.broadcasted_iota(jnp.int32, (bs, bs), 1)
    causal_bias = jnp.where(key_iota <= qry_iota, 0.0, NEG_INF)
    own_first = (i % 2) == 0
    t_own = i // 2
    r_own = pl.multiple_of(t_own * tile, tile)

    gates = [_nt(means_ref[hh].astype(BF16), q_ref[hh]) for hh in range(2)]
    own_scores = [_nt(k_ref[hh, pl.ds(r_own, tile), :], q_ref[hh]) for hh in range(2)]
    past = blk_iota < i
    for hh in range(2):
        gate = jnp.where(past, gates[hh], NEG_INF)
        chosen = jnp.zeros(gate.shape, jnp.bool_)
        for _ in range(MOBA_TOPK):
            mx = jnp.max(gate, axis=0, keepdims=True)
            idx = jnp.min(jnp.where(gate == mx, blk_f, float(nblk_pad)), axis=0, keepdims=True)
            pick = blk_f == idx
            chosen = chosen | pick
            gate = jnp.where(pick, NEG_INF, gate)
        bias_ref[hh] = jnp.where(chosen & past, 0.0, NEG_INF)

    def scores(t, slot):
        c0 = t * tile if isinstance(t, int) else pl.multiple_of(t * tile, tile)
        live = t < t_own
        for hh in range(2):
            s = _nt(k_ref[hh, pl.ds(c0, tile), :], q_ref[hh])
            r0 = jnp.where(live, bias_ref[hh, pl.ds(2 * t, 1), :], NEG_INF)
            r1 = jnp.where(live, bias_ref[hh, pl.ds(2 * t + 1, 1), :], NEG_INF)
            s_ref[slot, hh, :bs, :] = s[:bs] + r0
            s_ref[slot, hh, bs:, :] = s[bs:] + r1

    def softmax(slot):
        alphas = []
        for hh in range(2):
            s = s_ref[slot, hh]
            m = m_ref[hh]
            m_new = jnp.maximum(m, jnp.max(s, axis=0, keepdims=True))
            alphas.append(jnp.exp2(m - m_new))
            m_ref[hh] = m_new
            p_ref[slot, hh] = jnp.exp2((s - m_new).astype(BF16))
        return alphas

    def values(slot, v0, alphas):
        for hh in range(2):
            acc_ref[hh] = alphas[hh] * acc_ref[hh] + _mm(vt_ref[hh, :, pl.ds(v0, tile)], p_ref[slot, hh])

    scores(0, 0)
    for hh in range(2):
        prev = jnp.broadcast_to(bias_ref[hh, pl.ds(2 * t_own, 1), :], (bs, bs))
        first = jnp.where(own_first, causal_bias, prev)
        second = jnp.where(own_first, NEG_INF, causal_bias)
        s = own_scores[hh] + jnp.concatenate([first, second], axis=0)
        m = jnp.max(s, axis=0, keepdims=True)
        m_ref[hh] = m
        p_ref[1, hh] = jnp.exp2((s - m).astype(BF16))
        alpha_ref[hh] = jnp.zeros((1, bs), F32)
        acc_ref[hh] = jnp.zeros((MOBA_VROWS, bs), F32)
    n_pair = (t_own + 1) // 2
    last_tile = nblk // 2 - 1

    def body(u, carry):
        pending = jnp.where(u == 0, r_own, (2 * u - 1) * tile)
        values(1, pl.multiple_of(pending, tile), [alpha_ref[0], alpha_ref[1]])
        scores(2 * u + 1, 1)
        alphas = softmax(0)
        values(0, pl.multiple_of(2 * u * tile, tile), alphas)
        scores(jnp.minimum(2 * u + 2, last_tile), 0)
        alphas = softmax(1)
        alpha_ref[0] = alphas[0]
        alpha_ref[1] = alphas[1]
        return carry

    lax.fori_loop(0, n_pair, body, 0)
    pending = jnp.where(n_pair == 0, r_own, (2 * n_pair - 1) * tile)
    values(1, pl.multiple_of(pending, tile), [alpha_ref[0], alpha_ref[1]])
    dh = MOBA_HEAD_DIM
    out = jnp.concatenate([acc_ref[hh, :dh, :] / acc_ref[hh, dh:dh + 1, :] for hh in range(2)], axis=0)
    o_ref[...] = out.T


def _moba_prompt(qh, kh, vt):
    nseq, _, seqlen, dh = qh.shape
    bs = MOBA_BLOCK
    assert seqlen % (2 * bs) == 0
    nblk = seqlen // bs
    nblk_pad = -(-nblk // 8) * 8
    kern = functools.partial(_moba_prompt_kernel, nblk=nblk, nblk_pad=nblk_pad)
    return pl.pallas_call(
        kern,
        grid=(nseq, MOBA_HEADS // 2, nblk),
        in_specs=[pl.BlockSpec((None, 2, bs, dh), lambda b, h, i: (b, h, i, 0)),
                  pl.BlockSpec((None, 2, seqlen, dh), lambda b, h, i: (b, h, 0, 0)),
                  pl.BlockSpec((None, 2, MOBA_VROWS, seqlen), lambda b, h, i: (b, h, 0, 0))],
        out_specs=pl.BlockSpec((bs, 2 * dh), lambda b, h, i: (b * nblk + i, h)),
        out_shape=jax.ShapeDtypeStruct((nseq * seqlen, MOBA_WIDTH), F32),
        scratch_shapes=[pltpu.VMEM((2, nblk_pad, dh), F32), pltpu.VMEM((2, nblk_pad, bs), F32),
                        pltpu.VMEM((2, 1, bs), F32),
                        pltpu.VMEM((2, MOBA_VROWS, bs), F32), pltpu.VMEM((2, 1, bs), F32),
                        pltpu.VMEM((2, 2, 2 * bs, bs), F32), pltpu.VMEM((2, 2, 2 * bs, bs), BF16)],
        compiler_params=_cparams(("parallel", "parallel", "arbitrary")),
        name="moba_prompt",
    )(qh, kh, vt)


def _moba_gate_kernel(pt_ref, qbd_ref, *rest, ppg):
    del pt_ref
    k_refs = rest[:ppg]
    sel_ref, g_ref = rest[ppg:]
    s_id = pl.program_id(1)
    lane_blk = lax.broadcasted_iota(jnp.int32, (MOBA_HEADS, LANE), 1)

    @pl.when(s_id == 0)
    def _():
        g_ref[...] = jnp.full_like(g_ref, NEG_INF)

    qbd = qbd_ref[...].astype(BF16)
    g = g_ref[...]
    for bi in range(ppg // 2):
        s = jnp.zeros((MOBA_HEADS, PAGE_SIZE), F32)
        for pg in range(2):
            kt = k_refs[2 * bi + pg][...].reshape(MOBA_WIDTH, PAGE_SIZE)
            s = s + _mm(qbd, kt.astype(BF16))
        gate = jnp.sum(s, axis=1, keepdims=True)
        g = jnp.where(lane_blk == s_id * (ppg // 2) + bi, gate, g)
    g_ref[...] = g

    @pl.when(s_id == pl.num_programs(1) - 1)
    def _():
        gg = g_ref[...]
        lane_f = lane_blk.astype(F32)
        picked = jnp.zeros((MOBA_HEADS, LANE), F32)
        for r in range(MOBA_TOPK):
            mx = jnp.max(gg, axis=1, keepdims=True)
            idx = jnp.min(jnp.where(gg == mx, lane_f, float(LANE)), axis=1, keepdims=True)
            picked = jnp.where(lane_blk == r, idx, picked)
            gg = jnp.where(lane_f == idx, NEG_INF, gg)
        sel_ref[...] = picked.astype(jnp.int32)


PICK_HEADS = 4


def _moba_pick_kernel(pt_ref, sel_ref, q_ref, kn_ref, vn_ref, *rest):
    del pt_ref, sel_ref
    npg = 2 * MOBA_TOPK
    k_refs = rest[:PICK_HEADS * npg]
    v_refs = rest[PICK_HEADS * npg:2 * PICK_HEADS * npg]
    o_ref = rest[2 * PICK_HEADS * npg]
    scale = MOBA_HEAD_DIM ** -0.5
    sub = 8

    for hs in range(PICK_HEADS):
        q = q_ref[hs]
        q8 = jnp.broadcast_to((q * scale).astype(BF16), (sub, MOBA_HEAD_DIM))
        s_own = jnp.sum(q * kn_ref[hs], axis=1, keepdims=True) * scale
        s = [_mm(q8, k_refs[hs * npg + j][...].astype(BF16))[0:1, :] for j in range(npg)]
        m = s_own
        for sj in s:
            m = jnp.maximum(m, jnp.max(sj, axis=1, keepdims=True))
        p_own = jnp.exp(s_own - m)
        den = p_own
        num = p_own * vn_ref[hs]
        for j in range(npg):
            p = jnp.exp(s[j] - m)
            den = den + jnp.sum(p, axis=1, keepdims=True)
            p8 = jnp.broadcast_to(p.astype(BF16), (sub, PAGE_SIZE))
            num = num + _nt(p8, v_refs[hs * npg + j][...].astype(BF16))[0:1, :]
        o_ref[hs] = num / den


def _moba_decode(page_table, q, k_new, v_new, cache_kt, cache_vt, layer):
    nseq, n_pages = page_table.shape
    nblk = n_pages * PAGE_SIZE // MOBA_BLOCK
    assert n_pages * PAGE_SIZE == nblk * MOBA_BLOCK and MOBA_TOPK <= nblk <= LANE
    ppg = 16 if n_pages % 16 == 0 else 2
    heads, dh = MOBA_HEADS, MOBA_HEAD_DIM

    qbd = (q[:, :, None, :] * jnp.eye(heads, dtype=F32)[None, :, :, None]).reshape(nseq, heads, heads * dh)

    def all_heads(p):
        return pl.BlockSpec((None, None, heads, dh, PAGE_SIZE),
                            lambda b, s, pt, p=p: (pt[b, s * ppg + p], layer, 0, 0, 0))

    sel = pl.pallas_call(
        functools.partial(_moba_gate_kernel, ppg=ppg),
        grid_spec=pltpu.PrefetchScalarGridSpec(
            num_scalar_prefetch=1,
            grid=(nseq, n_pages // ppg),
            in_specs=[pl.BlockSpec((None, heads, heads * dh), lambda b, s, pt: (b, 0, 0))]
            + [all_heads(p) for p in range(ppg)],
            out_specs=pl.BlockSpec((None, heads, LANE), lambda b, s, pt: (b, 0, 0)),
            scratch_shapes=[pltpu.VMEM((heads, LANE), F32)],
        ),
        out_shape=jax.ShapeDtypeStruct((nseq, heads, LANE), jnp.int32),
        compiler_params=_cparams(("parallel", "arbitrary")),
        name="moba_gate",
    )(page_table, qbd, *([cache_kt] * ppg))
    sel_flat = sel[:, :, :MOBA_TOPK].reshape(nseq * heads * MOBA_TOPK)

    def one_head(hs, j):
        def index(b, g, pt, sl):
            h = g * PICK_HEADS + hs
            blk = sl[(b * heads + h) * MOBA_TOPK + j // 2]
            return (pt[b, 2 * blk + j % 2], layer, h, 0, 0)
        return pl.BlockSpec((None, None, None, dh, PAGE_SIZE), index)

    row = pl.BlockSpec((None, PICK_HEADS, 1, dh), lambda b, g, pt, sl: (b, g, 0, 0))
    rows = lambda a: a.reshape(nseq, heads, 1, dh)
    pages = [one_head(hs, j) for hs in range(PICK_HEADS) for j in range(2 * MOBA_TOPK)]
    out = pl.pallas_call(
        _moba_pick_kernel,
        grid_spec=pltpu.PrefetchScalarGridSpec(
            num_scalar_prefetch=2,
            grid=(nseq, heads // PICK_HEADS),
            in_specs=[row, row, row] + pages + pages,
            out_specs=row,
        ),
        out_shape=jax.ShapeDtypeStruct((nseq, heads, 1, dh), F32),
        compiler_params=_cparams(("parallel", "parallel")),
        name="moba_pick",
    )(page_table, sel_flat, rows(q), rows(k_new), rows(v_new),
      *([cache_kt] * len(pages)), *([cache_vt] * len(pages)))
    return out.reshape(nseq, heads * dh)


def _s5_disc_kernel(are_ref, aim_ref, ls_ref, bre_ref, bim_ref, abr_ref, abi_ref, bbr_ref, bbi_ref):
    a_re = are_ref[...]
    a_im = aim_ref[...]
    step = jnp.exp(ls_ref[...])
    mag = jnp.exp(a_re * step)
    ab_re = mag * jnp.cos(a_im * step)
    ab_im = mag * jnp.sin(a_im * step)
    den = a_re * a_re + a_im * a_im
    n_re = ab_re - 1.0
    f_re = (n_re * a_re + ab_im * a_im) / den
    f_im = (ab_im * a_re - n_re * a_im) / den
    b_re = bre_ref[...]
    b_im = bim_ref[...]
    abr_ref[...] = ab_re
    abi_ref[...] = ab_im
    bbr_ref[...] = f_re * b_re - f_im * b_im
    bbi_ref[...] = f_re * b_im + f_im * b_re


def _s5_discretise(a_re, a_im, log_step, b_re, b_im):
    depth = a_re.shape[0]
    rows = depth * S5_GROUPS * S5_GROUP
    rep = lambda a: jnp.broadcast_to(a[:, :, None, :], (depth, S5_GROUPS, S5_GROUP, S5_STATE)).reshape(rows, S5_STATE)
    ls = jnp.broadcast_to(log_step[:, :, None, None], (depth, S5_GROUPS, S5_GROUP, S5_STATE)).reshape(rows, S5_STATE)
    bt = lambda b: jnp.swapaxes(b, 2, 3).reshape(rows, S5_STATE)
    shp = jax.ShapeDtypeStruct((rows, S5_STATE), F32)
    abr, abi, bbr, bbi = pl.pallas_call(
        _s5_disc_kernel, out_shape=[shp] * 4, name="s5_disc",
    )(rep(a_re), rep(a_im), ls, bt(b_re), bt(b_im))
    r4 = lambda a: a.reshape(depth, S5_GROUPS, S5_GROUP, S5_STATE)
    ab = jnp.concatenate([r4(abr)[:, :, 0].reshape(depth, 1, S5_CH),
                          r4(abi)[:, :, 0].reshape(depth, 1, S5_CH)], axis=1)
    eye = jnp.eye(S5_GROUPS, dtype=F32)
    bd = lambda a: (r4(a)[:, :, :, None, :] * eye[None, :, None, :, None]).reshape(depth, S5_WIDTH, S5_CH)
    bb = jnp.concatenate([bd(bbr), bd(bbi)], axis=2)
    return ab, bb.astype(BF16)


def _s5_out_map(c_re, c_im):
    depth = c_re.shape[0]
    eye = jnp.eye(S5_GROUPS, dtype=F32)
    bd = lambda c: (jnp.swapaxes(c, 2, 3)[:, :, :, None, :] * eye[None, :, None, :, None]).reshape(depth, S5_CH, S5_WIDTH)
    return jnp.concatenate([bd(c_re), -bd(c_im)], axis=1).astype(BF16)


S5_SCAN_LANES = 512


def _s5_kernel(u_ref, bb_ref, ab_ref, h0_ref, cc_ref, d_ref, wglu_ref, bglu_ref,
               o_ref, hfin_ref, xs_ref, hc_ref, *, nb, tt):
    @pl.when(pl.program_id(0) == 0)
    def _():
        hc_ref[...] = h0_ref[...]

    u = u_ref[...]
    xs_ref[...] = _mm(u.astype(BF16), bb_ref[...])

    for c0 in range(0, S5_CH, S5_SCAN_LANES):
        re_sl = slice(c0, c0 + S5_SCAN_LANES)
        im_sl = slice(S5_CH + c0, S5_CH + c0 + S5_SCAN_LANES)
        ar = jnp.broadcast_to(ab_ref[0:1, re_sl], (nb, S5_SCAN_LANES))
        ai = jnp.broadcast_to(ab_ref[1:2, re_sl], (nb, S5_SCAN_LANES))

        def step(t, carry, re_sl=re_sl, im_sl=im_sl, ar=ar, ai=ai):
            hr, hi = carry
            rows = pl.ds(pl.multiple_of(t * nb, nb), nb)
            nhr = ar * hr - ai * hi + xs_ref[rows, re_sl]
            nhi = ar * hi + ai * hr + xs_ref[rows, im_sl]
            xs_ref[rows, re_sl] = nhr
            xs_ref[rows, im_sl] = nhi
            return nhr, nhi

        hr, hi = lax.fori_loop(0, tt, step, (hc_ref[:, re_sl], hc_ref[:, im_sl]), unroll=min(4, tt))
        hc_ref[:, re_sl] = hr
        hc_ref[:, im_sl] = hi

    y = _mm(xs_ref[...].astype(BF16), cc_ref[...]) + d_ref[...] * u
    yg = _gelu_tanh(y)
    gate = _sigmoid(_mm(yg.astype(BF16), wglu_ref[...]) + bglu_ref[...])
    o_ref[...] = yg * gate
    hfin_ref[...] = hc_ref[...]


def _s5(u_tm, nb, bb, ab, h0, cc, d, wglu, bglu):
    rows = u_tm.shape[0]
    t_total = rows // nb
    tt = min(64, t_total)
    const = lambda shape: pl.BlockSpec(shape, lambda t: (0,) * len(shape))
    kern = functools.partial(_s5_kernel, nb=nb, tt=tt)
    return pl.pallas_call(
        kern,
        grid=(t_total // tt,),
        in_specs=[pl.BlockSpec((tt * nb, S5_WIDTH), lambda t: (t, 0)),
                  const((S5_WIDTH, 2 * S5_CH)), const((2, S5_CH)), const((nb, 2 * S5_CH)),
                  const((2 * S5_CH, S5_WIDTH)), const((1, S5_WIDTH)),
                  const((S5_WIDTH, S5_WIDTH)), const((1, S5_WIDTH))],
        out_specs=[pl.BlockSpec((tt * nb, S5_WIDTH), lambda t: (t, 0)), const((nb, 2 * S5_CH))],
        out_shape=[jax.ShapeDtypeStruct((rows, S5_WIDTH), F32), jax.ShapeDtypeStruct((nb, 2 * S5_CH), F32)],
        scratch_shapes=[pltpu.VMEM((tt * nb, 2 * S5_CH), F32), pltpu.VMEM((nb, 2 * S5_CH), F32)],
        compiler_params=_cparams(("arbitrary",)),
        name="s5_scan",
    )(u_tm, bb, ab, h0, cc, d, wglu, bglu)


def _split3(x):
    hi = x.astype(BF16)
    r = x - hi.astype(F32)
    mid = r.astype(BF16)
    lo = (r - mid.astype(F32)).astype(BF16)
    return hi, mid, lo


def _gla_prompt_kernel(gq_ref, gk_ref, gv_ref, gr_ref, lf_ref, gn_ref, o_ref, sfin_ref, st_ref, *, nchunk):
    c = GLA_CHUNK

    @pl.when(pl.program_id(1) == 0)
    def _():
        st_ref[...] = jnp.zeros_like(st_ref)

    row = lax.broadcasted_iota(jnp.int32, (c, c), 0)
    col = lax.broadcasted_iota(jnp.int32, (c, c), 1)
    causal = col <= row
    tri = causal.astype(BF16)
    gn = gn_ref[...]

    for ci in range(nchunk):
        rows = slice(ci * c, (ci + 1) * c)
        g_hi, g_mid, g_lo = _split3(lf_ref[rows, :])
        bcum = _mm(tri, g_hi) + _mm(tri, g_mid) + _mm(tri, g_lo)
        blast = bcum[c - 1:c, :]
        qd = (gq_ref[rows, :] * jnp.exp(bcum)).astype(BF16)
        kd = (gk_ref[rows, :] * jnp.exp(-bcum)).astype(BF16)
        kt = (gk_ref[rows, :] * jnp.exp(blast - bcum)).astype(BF16)
        st = st_ref[...]
        st_ref[...] = st * jnp.exp(blast)
        st16 = st.astype(BF16)
        for hd in range(GLA_HEADS):
            ks = slice(hd * GLA_DK, (hd + 1) * GLA_DK)
            vs = slice(hd * GLA_DV, (hd + 1) * GLA_DV)
            vh = gv_ref[rows, vs].astype(BF16)
            att = jnp.where(causal, _nt(qd[:, ks], kd[:, ks]), 0.0)
            o = _nt(qd[:, ks], st16[:, ks]) + _mm(att.astype(BF16), vh)
            st_ref[:, ks] += _tn(vh, kt[:, ks])
            o_ref[rows, vs] = _rms(o, gn) * _silu(gr_ref[rows, vs])

    sfin_ref[...] = st_ref[...]


def _gla_prompt(gq, gk, gv, gr, lf, gnorm, nseq, seqlen):
    assert seqlen % GLA_CHUNK == 0
    tt = min(256, seqlen)
    nt = seqlen // tt
    tok = lambda width: pl.BlockSpec((tt, width), lambda b, t: (b * nt + t, 0))
    kern = functools.partial(_gla_prompt_kernel, nchunk=tt // GLA_CHUNK)
    return pl.pallas_call(
        kern,
        grid=(nseq, nt),
        in_specs=[tok(GLA_KEY_WIDTH), tok(GLA_KEY_WIDTH), tok(GLA_VAL_WIDTH), tok(GLA_VAL_WIDTH),
                  tok(GLA_KEY_WIDTH), pl.BlockSpec((1, GLA_DV), lambda b, t: (0, 0))],
        out_specs=[tok(GLA_VAL_WIDTH),
                   pl.BlockSpec((None, GLA_DV, GLA_KEY_WIDTH), lambda b, t: (b, 0, 0))],
        out_shape=[jax.ShapeDtypeStruct((nseq * seqlen, GLA_VAL_WIDTH), F32),
                   jax.ShapeDtypeStruct((nseq, GLA_DV, GLA_KEY_WIDTH), F32)],
        scratch_shapes=[pltpu.VMEM((GLA_DV, GLA_KEY_WIDTH), F32)],
        compiler_params=_cparams(("parallel", "arbitrary")),
        name="gla_prompt",
    )(gq, gk, gv, gr, lf, gnorm)


def _gla_decode_kernel(q_ref, k_ref, g_ref, v_ref, r_ref, s0_ref, gn_ref, o_ref, s_ref):
    s = jnp.exp(g_ref[...]) * s0_ref[...] + k_ref[...] * v_ref[...]
    s_ref[...] = s
    o = jnp.sum(q_ref[...] * s, axis=1, keepdims=True)
    o_ref[...] = _rms(o, gn_ref[...]) * _silu(r_ref[...])


def _gla_decode(gq, gk, lf, gv, gr, s0, gnorm):
    n = gq.shape[0]
    col = lambda a: a.reshape(n, GLA_HEADS, GLA_DK, 1)
    rowv = lambda a: a.reshape(n, GLA_HEADS, 1, GLA_DV)
    cspec = pl.BlockSpec((None, GLA_HEADS, GLA_DK, 1), lambda b: (b, 0, 0, 0))
    rspec = pl.BlockSpec((None, GLA_HEADS, 1, GLA_DV), lambda b: (b, 0, 0, 0))
    sspec = pl.BlockSpec((None, GLA_HEADS, GLA_DK, GLA_DV), lambda b: (b, 0, 0, 0))
    o, s = pl.pallas_call(
        _gla_decode_kernel,
        grid=(n,),
        in_specs=[cspec, cspec, cspec, rspec, rspec, sspec,
                  pl.BlockSpec((1, 1, GLA_DV), lambda b: (0, 0, 0))],
        out_specs=[rspec, sspec],
        out_shape=[jax.ShapeDtypeStruct((n, GLA_HEADS, 1, GLA_DV), F32),
                   jax.ShapeDtypeStruct((n, GLA_HEADS, GLA_DK, GLA_DV), F32)],
        compiler_params=_cparams(("parallel",)),
        name="gla_decode",
    )(col(gq), col(gk), col(lf), rowv(gv), rowv(gr), s0, gnorm.reshape(1, 1, GLA_DV))
    return o.reshape(n, GLA_VAL_WIDTH), s


def _out_ffn_kernel(x_ref, a_ref, b_ref, c_ref, wo_ref, nf_ref, wg_ref, wu_ref, wd_ref, nfin_ref,
                    o_ref, hf_ref, acc_ref, *, final_norm):
    j = pl.program_id(2)

    @pl.when(j == 0)
    def _():
        mix = jnp.concatenate([a_ref[...], b_ref[...], c_ref[...]], axis=1).astype(BF16)
        x1 = x_ref[...] + _mm(mix, wo_ref[...])
        acc_ref[...] = x1
        hf_ref[...] = _rms(x1, nf_ref[...]).astype(BF16)

    hf = hf_ref[...]
    act = _silu(_mm(hf, wg_ref[...])) * _mm(hf, wu_ref[...])
    acc_ref[...] += _mm(act.astype(BF16), wd_ref[...])

    @pl.when(j == pl.num_programs(2) - 1)
    def _():
        y = acc_ref[...]
        o_ref[...] = _rms(y, nfin_ref[...]) if final_norm else y


def _out_ffn(x2d, a, b_tm, c, nseq, seqlen, w_out, norm_ffn, w_gate, w_up, w_down, norm_final, final_norm):
    tm = min(512, seqlen)
    nt = seqlen // tm
    th = FFN_HIDDEN // 2
    nh = FFN_HIDDEN // th
    tok = lambda width: pl.BlockSpec((tm, width), lambda b, t, j: (b * nt + t, 0))
    const = lambda shape: pl.BlockSpec(shape, lambda b, t, j: (0,) * len(shape))
    kern = functools.partial(_out_ffn_kernel, final_norm=final_norm)
    return pl.pallas_call(
        kern,
        grid=(nseq, nt, nh),
        in_specs=[tok(D_MODEL), tok(MOBA_WIDTH),
                  pl.BlockSpec((tm, S5_WIDTH), lambda b, t, j: (t, b)),
                  tok(GLA_VAL_WIDTH), const((D_MODEL, D_MODEL)), const((1, D_MODEL)),
                  pl.BlockSpec((D_MODEL, th), lambda b, t, j: (0, j)),
                  pl.BlockSpec((D_MODEL, th), lambda b, t, j: (0, j)),
                  pl.BlockSpec((th, D_MODEL), lambda b, t, j: (j, 0)),
                  const((1, D_MODEL))],
        out_specs=tok(D_MODEL),
        out_shape=jax.ShapeDtypeStruct((nseq * seqlen, D_MODEL), F32),
        scratch_shapes=[pltpu.VMEM((tm, D_MODEL), BF16), pltpu.VMEM((tm, D_MODEL), F32)],
        compiler_params=_cparams(("parallel", "parallel", "arbitrary")),
        name="out_ffn",
    )(x2d, a, b_tm, c, w_out, norm_ffn, w_gate, w_up, w_down, norm_final)


def _prep_weights(w_in, w_out, s5_w_glu, gla_w_gate, w_ffn_gate, w_ffn_up, w_ffn_down):
    depth = w_in.shape[0]
    w_in_p = jnp.pad(w_in, ((0, 0), (0, 0), (0, IN_COLS_PAD - IN_COLS))).astype(BF16)
    wg_p = jnp.pad(gla_w_gate, ((0, 0), (0, LANE - GLA_GATE_RANK), (0, 0))).astype(BF16)
    del depth
    return (w_in_p, w_out.astype(BF16), s5_w_glu.astype(BF16), wg_p,
            w_ffn_gate.astype(BF16), w_ffn_up.astype(BF16), w_ffn_down.astype(BF16))


def _trunk(x, kv_pool, ssm_re0, ssm_im0, gla0, params):
    (norm_mix, w_in_p, w_out, ab, bb, cc, s5_d, w_glu, s5_b_glu, wg_p, gla_b_gate, gla_norm,
     norm_ffn, w_gate, w_up, w_down, norm_final) = params
    nseq, seqlen, _ = x.shape
    depth = w_in_p.shape[0]
    decode = kv_pool is not None
    if decode:
        assert seqlen == 1
        cache_k, cache_v, page_table = kv_pool
        cache_kt = jnp.transpose(cache_k, (0, 1, 3, 4, 2))
        cache_vt = jnp.transpose(cache_v, (0, 1, 3, 4, 2))
        tw_nseq, tw_len = 1, nseq
    else:
        assert seqlen % MOBA_BLOCK == 0
        tw_nseq, tw_len = nseq, seqlen
    n = nseq * seqlen
    x2d = x.reshape(n, D_MODEL)
    k_rows, v_rows, s_re, s_im, s_gla = [], [], [], [], []
    for l in range(depth):
        k, v, *q_parts, u_tm, gq, gk, gv, gr, lf = _in_proj(
            x2d, tw_nseq, tw_len, norm_mix[l][None], w_in_p[l], wg_p[l], gla_b_gate[l][None],
            head_major=not decode)
        k_rows.append(k)
        v_rows.append(v)
        if decode:
            heads = lambda a: a.reshape(nseq, MOBA_HEADS, MOBA_HEAD_DIM)
            a_out = _moba_decode(page_table, heads(q_parts[0]), heads(k), heads(v), cache_kt, cache_vt, l)
        else:
            a_out = _moba_prompt(*q_parts)
        if ssm_re0 is None:
            h0 = jnp.zeros((nseq, 2 * S5_CH), F32)
        else:
            h0 = jnp.concatenate([ssm_re0[l].reshape(nseq, S5_CH), ssm_im0[l].reshape(nseq, S5_CH)], axis=1)
        b_tm, h_fin = _s5(u_tm.reshape(n, S5_WIDTH), nseq, bb[l], ab[l], h0, cc[l], s5_d[l][None],
                          w_glu[l], s5_b_glu[l][None])
        s_re.append(h_fin[:, :S5_CH].reshape(nseq, S5_GROUPS, S5_STATE))
        s_im.append(h_fin[:, S5_CH:].reshape(nseq, S5_GROUPS, S5_STATE))
        b_tm = b_tm.reshape(tw_len, tw_nseq * S5_WIDTH)
        if decode:
            c_out, s_fin = _gla_decode(gq, gk, lf, gv, gr, gla0[l], gla_norm[l])
        else:
            c_out, s_t = _gla_prompt(gq, gk, gv, gr, lf, gla_norm[l][None], nseq, seqlen)
            s_fin = jnp.transpose(s_t.reshape(nseq, GLA_DV, GLA_HEADS, GLA_DK), (0, 2, 3, 1))
        s_gla.append(s_fin)
        x2d = _out_ffn(x2d, a_out, b_tm, c_out, tw_nseq, tw_len, w_out[l], norm_ffn[l][None],
                       w_gate[l], w_up[l], w_down[l], norm_final[None], final_norm=(l == depth - 1))
    if decode:
        shp = (nseq, seqlen, MOBA_HEADS, MOBA_HEAD_DIM)
        stack_kv = lambda rows: jnp.stack([r.reshape(shp) for r in rows], axis=1)
    else:
        shp = (nseq, MOBA_HEADS, MOBA_HEAD_DIM, seqlen)
        stack_kv = lambda rows: jnp.transpose(jnp.stack([r.reshape(shp) for r in rows], axis=1), (0, 1, 4, 2, 3))
    return (x2d.reshape(nseq, seqlen, D_MODEL), stack_kv(k_rows), stack_kv(v_rows),
            jnp.stack(s_re, axis=0), jnp.stack(s_im, axis=0), jnp.stack(s_gla, axis=0))


def kernel(x_prompt, x_sample, cache_k, cache_v, page_table, state_ssm_re, state_ssm_im, state_gla,
           norm_mix, w_in, w_out, s5_a_re, s5_a_im, s5_log_step, s5_b_re, s5_b_im, s5_c_re, s5_c_im,
           s5_d, s5_w_glu, s5_b_glu, gla_w_gate, gla_b_gate, gla_norm, norm_ffn, w_ffn_gate, w_ffn_up,
           w_ffn_down, norm_final):
    w_in_p, w_out16, w_glu16, wg_p, w_gate16, w_up16, w_down16 = _prep_weights(
        w_in, w_out, s5_w_glu, gla_w_gate, w_ffn_gate, w_ffn_up, w_ffn_down)
    ab, bb = _s5_discretise(s5_a_re, s5_a_im, s5_log_step, s5_b_re, s5_b_im)
    cc = _s5_out_map(s5_c_re, s5_c_im)
    params = (norm_mix, w_in_p, w_out16, ab, bb, cc, s5_d, w_glu16, s5_b_glu, wg_p, gla_b_gate, gla_norm,
              norm_ffn, w_gate16, w_up16, w_down16, norm_final)
    y_p, k_p, v_p, re_p, im_p, gla_p = _trunk(x_prompt, None, None, None, None, params)
    y_s, k_s, v_s, re_s, im_s, gla_s = _trunk(x_sample, (cache_k, cache_v, page_table),
                                              state_ssm_re, state_ssm_im, state_gla, params)
    return (y_p, y_s, k_p, v_p, k_s, v_s, re_p, im_p, re_s, im_s, gla_p, gla_s)
```

```python
import functools
import math

import jax
import jax.numpy as jnp
from jax import lax
from jax.experimental import pallas as pl
from jax.experimental.pallas import tpu as pltpu

F32 = jnp.float32
BF16 = jnp.bfloat16

D_MODEL = 1024
PAGE_SIZE = 128
MOBA_HEAD_DIM = 64
MOBA_HEADS = 8
MOBA_WIDTH = MOBA_HEADS * MOBA_HEAD_DIM
MOBA_BLOCK = 256
MOBA_TOPK = 3
MOBA_VROWS = 80
LOG2E = 1.4426950408889634
S5_WIDTH = 256
S5_GROUP = 16
S5_GROUPS = 16
S5_STATE = 64
S5_CH = S5_GROUPS * S5_STATE
GLA_HEADS = 4
GLA_DK = 32
GLA_DV = 64
GLA_KEY_WIDTH = GLA_HEADS * GLA_DK
GLA_VAL_WIDTH = GLA_HEADS * GLA_DV
GLA_GATE_RANK = 16
GLA_GATE_NORM = 16.0
GLA_CHUNK = 64
FFN_HIDDEN = 2816
RMS_EPS = 1e-6
NEG_INF = float("-inf")

_OFF_Q = 0
_OFF_K = _OFF_Q + MOBA_WIDTH
_OFF_V = _OFF_K + MOBA_WIDTH
_OFF_U = _OFF_V + MOBA_WIDTH
_OFF_GQ = _OFF_U + S5_WIDTH
_OFF_GK = _OFF_GQ + GLA_KEY_WIDTH
_OFF_GV = _OFF_GK + GLA_KEY_WIDTH
_OFF_GR = _OFF_GV + GLA_VAL_WIDTH
_OFF_GLR = _OFF_GR + GLA_VAL_WIDTH
IN_COLS = _OFF_GLR + GLA_GATE_RANK
LANE = 128
IN_COLS_PAD = _OFF_GLR + LANE

VMEM_LIMIT = 56 * 1024 * 1024


def _cparams(sem):
    return pltpu.CompilerParams(dimension_semantics=sem, vmem_limit_bytes=VMEM_LIMIT)


def _nt(a, b):
    return lax.dot_general(a, b, (((1,), (1,)), ((), ())), preferred_element_type=F32)


def _tn(a, b):
    return lax.dot_general(a, b, (((0,), (0,)), ((), ())), preferred_element_type=F32)


def _mm(a, b):
    return jnp.dot(a, b, preferred_element_type=F32)


def _sigmoid(x):
    return 1.0 / (1.0 + jnp.exp(-x))


def _silu(x):
    return x * _sigmoid(x)


def _gelu_tanh(x):
    c = math.sqrt(2.0 / math.pi)
    return 0.5 * x * (1.0 + jnp.tanh(c * (x + 0.044715 * (x * x * x))))


def _log_sigmoid(x):
    return jnp.minimum(x, 0.0) - jnp.log(1.0 + jnp.exp(-jnp.abs(x)))


def _rms(x, g):
    return x * lax.rsqrt(jnp.mean(x * x, axis=-1, keepdims=True) + RMS_EPS) * g


def _in_proj_kernel(x_ref, g_ref, w_ref, wg_ref, bg_ref, k_ref, v_ref, *rest, head_major):
    h = _rms(x_ref[...], g_ref[...]).astype(BF16)

    def proj(lo, width):
        return _mm(h, w_ref[:, lo:lo + width])

    q = proj(_OFF_Q, MOBA_WIDTH)
    k = proj(_OFF_K, MOBA_WIDTH)
    v = proj(_OFF_V, MOBA_WIDTH)
    if head_major:
        vt = v.T
        k_ref[...] = k.T
        v_ref[...] = vt
        qh_ref, kh_ref, vt_ref = rest[:3]
        rest = rest[3:]
        q16 = (q * (MOBA_HEAD_DIM ** -0.5 * LOG2E)).astype(BF16)
        k16 = k.astype(BF16)
        vt16 = vt.astype(BF16)
        for hd in range(MOBA_HEADS):
            sl = slice(hd * MOBA_HEAD_DIM, (hd + 1) * MOBA_HEAD_DIM)
            qh_ref[hd] = q16[:, sl]
            kh_ref[hd] = k16[:, sl]
            vt_ref[hd, :MOBA_HEAD_DIM, :] = vt16[sl, :]
            vt_ref[hd, MOBA_HEAD_DIM:, :] = jnp.ones((MOBA_VROWS - MOBA_HEAD_DIM, vt16.shape[1]), BF16)
    else:
        k_ref[...] = k
        v_ref[...] = v
        rest[0][...] = q
        rest = rest[1:]
    u_ref, gq_ref, gk_ref, gv_ref, gr_ref, lf_ref = rest
    u_ref[...] = proj(_OFF_U, S5_WIDTH)
    gq_ref[...] = proj(_OFF_GQ, GLA_KEY_WIDTH) * (GLA_DK ** -0.5)
    gk_ref[...] = proj(_OFF_GK, GLA_KEY_WIDTH)
    gv_ref[...] = proj(_OFF_GV, GLA_VAL_WIDTH)
    gr_ref[...] = proj(_OFF_GR, GLA_VAL_WIDTH)
    glr = proj(_OFF_GLR, LANE)
    gate = _mm(glr.astype(BF16), wg_ref[...]) + bg_ref[...]
    lf_ref[...] = _log_sigmoid(gate) * (1.0 / GLA_GATE_NORM)


def _in_proj(x2d, nseq, seqlen, norm_g, w_pad, wg_pad, bg, head_major):
    n = nseq * seqlen
    tm = min(512, seqlen)
    nt = seqlen // tm
    tok = lambda width: pl.BlockSpec((tm, width), lambda b, t: (b * nt + t, 0))
    const = lambda shape: pl.BlockSpec(shape, lambda b, t: (0,) * len(shape))
    tokf = lambda width: jax.ShapeDtypeStruct((n, width), F32)
    if head_major:
        head = pl.BlockSpec((None, MOBA_HEADS, tm, MOBA_HEAD_DIM), lambda b, t: (b, 0, t, 0))
        head_t = pl.BlockSpec((None, MOBA_HEADS, MOBA_VROWS, tm), lambda b, t: (b, 0, 0, t))
        kv_t = pl.BlockSpec((None, MOBA_WIDTH, tm), lambda b, t: (b, 0, t))
        q_specs = [kv_t, kv_t, head, head, head_t]
        q_shapes = [jax.ShapeDtypeStruct((nseq, MOBA_WIDTH, seqlen), F32)] * 2 + [
            jax.ShapeDtypeStruct((nseq, MOBA_HEADS, seqlen, MOBA_HEAD_DIM), BF16)] * 2 + [
            jax.ShapeDtypeStruct((nseq, MOBA_HEADS, MOBA_VROWS, seqlen), BF16)]
    else:
        q_specs = [tok(MOBA_WIDTH), tok(MOBA_WIDTH), tok(MOBA_WIDTH)]
        q_shapes = [tokf(MOBA_WIDTH)] * 3
    return pl.pallas_call(
        functools.partial(_in_proj_kernel, head_major=head_major),
        grid=(nseq, nt),
        in_specs=[tok(D_MODEL), const((1, D_MODEL)), const((D_MODEL, IN_COLS_PAD)),
                  const((LANE, GLA_KEY_WIDTH)), const((1, GLA_KEY_WIDTH))],
        out_specs=q_specs + [
            pl.BlockSpec((tm, S5_WIDTH), lambda b, t: (t, b)),
            tok(GLA_KEY_WIDTH), tok(GLA_KEY_WIDTH), tok(GLA_VAL_WIDTH), tok(GLA_VAL_WIDTH),
            tok(GLA_KEY_WIDTH)],
        out_shape=q_shapes + [
            jax.ShapeDtypeStruct((seqlen, nseq * S5_WIDTH), F32),
            tokf(GLA_KEY_WIDTH), tokf(GLA_KEY_WIDTH), tokf(GLA_VAL_WIDTH), tokf(GLA_VAL_WIDTH),
            tokf(GLA_KEY_WIDTH)],
        compiler_params=_cparams(("parallel", "parallel")),
        name="in_proj",
    )(x2d, norm_g, w_pad, wg_pad, bg)


MOBA_STEP_HEADS = 4


def _moba_prompt_kernel(q_ref, k_ref, vt_ref, o_ref, means_ref, bias_ref, m_ref, acc_ref, alpha_ref, tmax_ref,
                        s_ref, p_ref, *, nblk, nblk_pad):
    i = pl.program_id(2)
    bs = MOBA_BLOCK
    heads = range(MOBA_STEP_HEADS)
    tile = 2 * bs

    @pl.when(i == 0)
    def _():
        means_ref[...] = jnp.zeros_like(means_ref)
        for hh in heads:
            for j in range(nblk):
                kj = k_ref[hh, j * bs:(j + 1) * bs, :].astype(F32)
                means_ref[hh, j:j + 1, :] = jnp.mean(kj, axis=0, keepdims=True)

    blk_iota = lax.broadcasted_iota(jnp.int32, (nblk_pad, bs), 0)
    blk_f = blk_iota.astype(F32)
    key_iota = lax.broadcasted_iota(jnp.int32, (bs, bs), 0)
    qry_iota = lax.broadcasted_iota(jnp.int32, (bs, bs), 1)
    causal_bias = jnp.where(key_iota <= qry_iota, 0.0, NEG_INF)
    own_first = (i % 2) == 0
    t_own = i // 2
    r_own = pl.multiple_of(t_own * tile, tile)

    gates = [_nt(means_ref[hh].astype(BF16), q_ref[hh]) for hh in heads]
    own_scores = [_nt(k_ref[hh, pl.ds(r_own, tile), :], q_ref[hh]) for hh in heads]
    past = blk_iota < i
    for hh in heads:
        gate = jnp.where(past, gates[hh], NEG_INF)
        chosen = jnp.zeros(gate.shape, jnp.bool_)
        for _ in range(MOBA_TOPK):
            mx = jnp.max(gate, axis=0, keepdims=True)
            idx = jnp.min(jnp.where(gate == mx, blk_f, float(nblk_pad)), axis=0, keepdims=True)
            pick = blk_f == idx
            chosen = chosen | pick
            gate = jnp.where(pick, NEG_INF, gate)
        bias_ref[hh] = jnp.where(chosen & past, 0.0, NEG_INF)

    def scores(t, slot):
        c0 = t * tile if isinstance(t, int) else pl.multiple_of(t * tile, tile)
        live = t < t_own
        for hh in heads:
            s = _nt(k_ref[hh, pl.ds(c0, tile), :], q_ref[hh])
            r0 = jnp.where(live, bias_ref[hh, pl.ds(2 * t, 1), :], NEG_INF)
            r1 = jnp.where(live, bias_ref[hh, pl.ds(2 * t + 1, 1), :], NEG_INF)
            s0 = s[:bs] + r0
            s1 = s[bs:] + r1
            s_ref[slot, hh, :bs, :] = s0
            s_ref[slot, hh, bs:, :] = s1
            tmax_ref[slot, hh] = jnp.max(jnp.maximum(s0, s1), axis=0, keepdims=True)

    def softmax(slot):
        for hh in heads:
            s = s_ref[slot, hh]
            m = m_ref[hh]
            m_new = jnp.maximum(m, tmax_ref[slot, hh])
            alpha_ref[slot, hh] = jnp.exp2(m - m_new)
            m_ref[hh] = m_new
            p_ref[slot, hh] = jnp.exp2((s - m_new).astype(BF16))

    def values(slot, v0):
        for hh in heads:
            pv = _mm(vt_ref[hh, :, pl.ds(v0, tile)], p_ref[slot, hh])
            acc_ref[hh] = alpha_ref[slot, hh] * acc_ref[hh] + pv

    scores(0, 0)
    for hh in heads:
        prev = jnp.broadcast_to(bias_ref[hh, pl.ds(2 * t_own, 1), :], (bs, bs))
        first = jnp.where(own_first, causal_bias, prev)
        second = jnp.where(own_first, NEG_INF, causal_bias)
        s = own_scores[hh] + jnp.concatenate([first, second], axis=0)
        m = jnp.max(s, axis=0, keepdims=True)
        m_ref[hh] = m
        p_ref[1, hh] = jnp.exp2((s - m).astype(BF16))
        alpha_ref[1, hh] = jnp.zeros((1, bs), F32)
        acc_ref[hh] = jnp.zeros((MOBA_VROWS, bs), F32)
    n_pair = (t_own + 1) // 2
    last_tile = nblk // 2 - 1

    def body(u, carry):
        pending = jnp.where(u == 0, r_own, (2 * u - 1) * tile)
        values(1, pl.multiple_of(pending, tile))
        scores(2 * u + 1, 1)
        softmax(0)
        values(0, pl.multiple_of(2 * u * tile, tile))
        scores(jnp.minimum(2 * u + 2, last_tile), 0)
        softmax(1)
        return carry

    lax.fori_loop(0, n_pair, body, 0)
    pending = jnp.where(n_pair == 0, r_own, (2 * n_pair - 1) * tile)
    values(1, pl.multiple_of(pending, tile))
    dh = MOBA_HEAD_DIM
    out = jnp.concatenate([acc_ref[hh, :dh, :] / acc_ref[hh, dh:dh + 1, :] for hh in heads], axis=0)
    o_ref[...] = out.T


def _moba_prompt(qh, kh, vt):
    nseq, _, seqlen, dh = qh.shape
    bs = MOBA_BLOCK
    assert seqlen % (2 * bs) == 0
    nblk = seqlen // bs
    nblk_pad = -(-nblk // 8) * 8
    nh = MOBA_STEP_HEADS
    kern = functools.partial(_moba_prompt_kernel, nblk=nblk, nblk_pad=nblk_pad)
    return pl.pallas_call(
        kern,
        grid=(nseq, MOBA_HEADS // nh, nblk),
        in_specs=[pl.BlockSpec((None, nh, bs, dh), lambda b, h, i: (b, h, i, 0)),
                  pl.BlockSpec((None, nh, seqlen, dh), lambda b, h, i: (b, h, 0, 0)),
                  pl.BlockSpec((None, nh, MOBA_VROWS, seqlen), lambda b, h, i: (b, h, 0, 0))],
        out_specs=pl.BlockSpec((bs, nh * dh), lambda b, h, i: (b * nblk + i, h)),
        out_shape=jax.ShapeDtypeStruct((nseq * seqlen, MOBA_WIDTH), F32),
        scratch_shapes=[pltpu.VMEM((nh, nblk_pad, dh), F32), pltpu.VMEM((nh, nblk_pad, bs), F32),
                        pltpu.VMEM((nh, 1, bs), F32),
                        pltpu.VMEM((nh, MOBA_VROWS, bs), F32), pltpu.VMEM((2, nh, 1, bs), F32),
                        pltpu.VMEM((2, nh, 1, bs), F32),
                        pltpu.VMEM((2, nh, 2 * bs, bs), F32), pltpu.VMEM((2, nh, 2 * bs, bs), BF16)],
        compiler_params=_cparams(("parallel", "parallel", "arbitrary")),
        name="moba_prompt",
    )(qh, kh, vt)


def _moba_gate_kernel(pt_ref, qbd_ref, *rest, ppg):
    del pt_ref
    k_refs = rest[:ppg]
    sel_ref, g_ref = rest[ppg:]
    s_id = pl.program_id(1)
    lane_blk = lax.broadcasted_iota(jnp.int32, (MOBA_HEADS, LANE), 1)

    @pl.when(s_id == 0)
    def _():
        g_ref[...] = jnp.full_like(g_ref, NEG_INF)

    qbd = qbd_ref[...].astype(BF16)
    g = g_ref[...]
    for bi in range(ppg // 2):
        s = jnp.zeros((MOBA_HEADS, PAGE_SIZE), F32)
        for pg in range(2):
            kt = k_refs[2 * bi + pg][...].reshape(MOBA_WIDTH, PAGE_SIZE)
            s = s + _mm(qbd, kt.astype(BF16))
        gate = jnp.sum(s, axis=1, keepdims=True)
        g = jnp.where(lane_blk == s_id * (ppg // 2) + bi, gate, g)
    g_ref[...] = g

    @pl.when(s_id == pl.num_programs(1) - 1)
    def _():
        gg = g_ref[...]
        lane_f = lane_blk.astype(F32)
        picked = jnp.zeros((MOBA_HEADS, LANE), F32)
        for r in range(MOBA_TOPK):
            mx = jnp.max(gg, axis=1, keepdims=True)
            idx = jnp.min(jnp.where(gg == mx, lane_f, float(LANE)), axis=1, keepdims=True)
            picked = jnp.where(lane_blk == r, idx, picked)
            gg = jnp.where(lane_f == idx, NEG_INF, gg)
        sel_ref[...] = picked.astype(jnp.int32)


PICK_HEADS = 4


def _moba_pick_kernel(pt_ref, sel_ref, q_ref, kn_ref, vn_ref, *rest):
    del pt_ref, sel_ref
    npg = 2 * MOBA_TOPK
    k_refs = rest[:PICK_HEADS * npg]
    v_refs = rest[PICK_HEADS * npg:2 * PICK_HEADS * npg]
    o_ref = rest[2 * PICK_HEADS * npg]
    scale = MOBA_HEAD_DIM ** -0.5
    sub = 8

    for hs in range(PICK_HEADS):
        q = q_ref[hs]
        q8 = jnp.broadcast_to((q * scale).astype(BF16), (sub, MOBA_HEAD_DIM))
        s_own = jnp.sum(q * kn_ref[hs], axis=1, keepdims=True) * scale
        s = [_mm(q8, k_refs[hs * npg + j][...].astype(BF16))[0:1, :] for j in range(npg)]
        m = s_own
        for sj in s:
            m = jnp.maximum(m, jnp.max(sj, axis=1, keepdims=True))
        p_own = jnp.exp(s_own - m)
        den = p_own
        num = p_own * vn_ref[hs]
        for j in range(npg):
            p = jnp.exp(s[j] - m)
            den = den + jnp.sum(p, axis=1, keepdims=True)
            p8 = jnp.broadcast_to(p.astype(BF16), (sub, PAGE_SIZE))
            num = num + _nt(p8, v_refs[hs * npg + j][...].astype(BF16))[0:1, :]
        o_ref[hs] = num / den


def _moba_decode(page_table, q, k_new, v_new, cache_kt, cache_vt, layer):
    nseq, n_pages = page_table.shape
    nblk = n_pages * PAGE_SIZE // MOBA_BLOCK
    assert n_pages * PAGE_SIZE == nblk * MOBA_BLOCK and MOBA_TOPK <= nblk <= LANE
    ppg = 16 if n_pages % 16 == 0 else 2
    heads, dh = MOBA_HEADS, MOBA_HEAD_DIM

    qbd = (q[:, :, None, :] * jnp.eye(heads, dtype=F32)[None, :, :, None]).reshape(nseq, heads, heads * dh)

    def all_heads(p):
        return pl.BlockSpec((None, None, heads, dh, PAGE_SIZE),
                            lambda b, s, pt, p=p: (pt[b, s * ppg + p], layer, 0, 0, 0))

    sel = pl.pallas_call(
        functools.partial(_moba_gate_kernel, ppg=ppg),
        grid_spec=pltpu.PrefetchScalarGridSpec(
            num_scalar_prefetch=1,
            grid=(nseq, n_pages // ppg),
            in_specs=[pl.BlockSpec((None, heads, heads * dh), lambda b, s, pt: (b, 0, 0))]
            + [all_heads(p) for p in range(ppg)],
            out_specs=pl.BlockSpec((None, heads, LANE), lambda b, s, pt: (b, 0, 0)),
            scratch_shapes=[pltpu.VMEM((heads, LANE), F32)],
        ),
        out_shape=jax.ShapeDtypeStruct((nseq, heads, LANE), jnp.int32),
        compiler_params=_cparams(("parallel", "arbitrary")),
        name="moba_gate",
    )(page_table, qbd, *([cache_kt] * ppg))
    sel_flat = sel[:, :, :MOBA_TOPK].reshape(nseq * heads * MOBA_TOPK)

    def one_head(hs, j):
        def index(b, g, pt, sl):
            h = g * PICK_HEADS + hs
            blk = sl[(b * heads + h) * MOBA_TOPK + j // 2]
            return (pt[b, 2 * blk + j % 2], layer, h, 0, 0)
        return pl.BlockSpec((None, None, None, dh, PAGE_SIZE), index)

    row = pl.BlockSpec((None, PICK_HEADS, 1, dh), lambda b, g, pt, sl: (b, g, 0, 0))
    rows = lambda a: a.reshape(nseq, heads, 1, dh)
    pages = [one_head(hs, j) for hs in range(PICK_HEADS) for j in range(2 * MOBA_TOPK)]
    out = pl.pallas_call(
        _moba_pick_kernel,
        grid_spec=pltpu.PrefetchScalarGridSpec(
            num_scalar_prefetch=2,
            grid=(nseq, heads // PICK_HEADS),
            in_specs=[row, row, row] + pages + pages,
            out_specs=row,
        ),
        out_shape=jax.ShapeDtypeStruct((nseq, heads, 1, dh), F32),
        compiler_params=_cparams(("parallel", "parallel")),
        name="moba_pick",
    )(page_table, sel_flat, rows(q), rows(k_new), rows(v_new),
      *([cache_kt] * len(pages)), *([cache_vt] * len(pages)))
    return out.reshape(nseq, heads * dh)


def _s5_disc_kernel(are_ref, aim_ref, ls_ref, bre_ref, bim_ref, abr_ref, abi_ref, bbr_ref, bbi_ref):
    a_re = are_ref[...]
    a_im = aim_ref[...]
    step = jnp.exp(ls_ref[...])
    mag = jnp.exp(a_re * step)
    ab_re = mag * jnp.cos(a_im * step)
    ab_im = mag * jnp.sin(a_im * step)
    den = a_re * a_re + a_im * a_im
    n_re = ab_re - 1.0
    f_re = (n_re * a_re + ab_im * a_im) / den
    f_im = (ab_im * a_re - n_re * a_im) / den
    b_re = bre_ref[...]
    b_im = bim_ref[...]
    abr_ref[...] = ab_re
    abi_ref[...] = ab_im
    bbr_ref[...] = f_re * b_re - f_im * b_im
    bbi_ref[...] = f_re * b_im + f_im * b_re


def _s5_discretise(a_re, a_im, log_step, b_re, b_im):
    depth = a_re.shape[0]
    rows = depth * S5_GROUPS * S5_GROUP
    rep = lambda a: jnp.broadcast_to(a[:, :, None, :], (depth, S5_GROUPS, S5_GROUP, S5_STATE)).reshape(rows, S5_STATE)
    ls = jnp.broadcast_to(log_step[:, :, None, None], (depth, S5_GROUPS, S5_GROUP, S5_STATE)).reshape(rows, S5_STATE)
    bt = lambda b: jnp.swapaxes(b, 2, 3).reshape(rows, S5_STATE)
    shp = jax.ShapeDtypeStruct((rows, S5_STATE), F32)
    abr, abi, bbr, bbi = pl.pallas_call(
        _s5_disc_kernel, out_shape=[shp] * 4, name="s5_disc",
    )(rep(a_re), rep(a_im), ls, bt(b_re), bt(b_im))
    r4 = lambda a: a.reshape(depth, S5_GROUPS, S5_GROUP, S5_STATE)
    ab = jnp.concatenate([r4(abr)[:, :, 0].reshape(depth, 1, S5_CH),
                          r4(abi)[:, :, 0].reshape(depth, 1, S5_CH)], axis=1)
    eye = jnp.eye(S5_GROUPS, dtype=F32)
    bd = lambda a: (r4(a)[:, :, :, None, :] * eye[None, :, None, :, None]).reshape(depth, S5_WIDTH, S5_CH)
    bb = jnp.concatenate([bd(bbr), bd(bbi)], axis=2)
    return ab, bb.astype(BF16)


def _s5_out_map(c_re, c_im):
    depth = c_re.shape[0]
    eye = jnp.eye(S5_GROUPS, dtype=F32)
    bd = lambda c: (jnp.swapaxes(c, 2, 3)[:, :, :, None, :] * eye[None, :, None, :, None]).reshape(depth, S5_CH, S5_WIDTH)
    return jnp.concatenate([bd(c_re), -bd(c_im)], axis=1).astype(BF16)


S5_SCAN_LANES = 512


def _s5_kernel(u_ref, bb_ref, ab_ref, h0_ref, cc_ref, d_ref, wglu_ref, bglu_ref,
               o_ref, hfin_ref, xs_ref, hc_ref, *, nb, tt):
    @pl.when(pl.program_id(0) == 0)
    def _():
        hc_ref[...] = h0_ref[...]

    u = u_ref[...]
    xs_ref[...] = _mm(u.astype(BF16), bb_ref[...])

    for c0 in range(0, S5_CH, S5_SCAN_LANES):
        re_sl = slice(c0, c0 + S5_SCAN_LANES)
        im_sl = slice(S5_CH + c0, S5_CH + c0 + S5_SCAN_LANES)
        ar = jnp.broadcast_to(ab_ref[0:1, re_sl], (nb, S5_SCAN_LANES))
        ai = jnp.broadcast_to(ab_ref[1:2, re_sl], (nb, S5_SCAN_LANES))

        def step(t, carry, re_sl=re_sl, im_sl=im_sl, ar=ar, ai=ai):
            hr, hi = carry
            rows = pl.ds(pl.multiple_of(t * nb, nb), nb)
            nhr = ar * hr - ai * hi + xs_ref[rows, re_sl]
            nhi = ar * hi + ai * hr + xs_ref[rows, im_sl]
            xs_ref[rows, re_sl] = nhr
            xs_ref[rows, im_sl] = nhi
            return nhr, nhi

        hr, hi = lax.fori_loop(0, tt, step, (hc_ref[:, re_sl], hc_ref[:, im_sl]), unroll=min(4, tt))
        hc_ref[:, re_sl] = hr
        hc_ref[:, im_sl] = hi

    y = _mm(xs_ref[...].astype(BF16), cc_ref[...]) + d_ref[...] * u
    yg = _gelu_tanh(y)
    gate = _sigmoid(_mm(yg.astype(BF16), wglu_ref[...]) + bglu_ref[...])
    o_ref[...] = yg * gate
    hfin_ref[...] = hc_ref[...]


def _s5(u_tm, nb, bb, ab, h0, cc, d, wglu, bglu):
    rows = u_tm.shape[0]
    t_total = rows // nb
    tt = min(64, t_total)
    const = lambda shape: pl.BlockSpec(shape, lambda t: (0,) * len(shape))
    kern = functools.partial(_s5_kernel, nb=nb, tt=tt)
    return pl.pallas_call(
        kern,
        grid=(t_total // tt,),
        in_specs=[pl.BlockSpec((tt * nb, S5_WIDTH), lambda t: (t, 0)),
                  const((S5_WIDTH, 2 * S5_CH)), const((2, S5_CH)), const((nb, 2 * S5_CH)),
                  const((2 * S5_CH, S5_WIDTH)), const((1, S5_WIDTH)),
                  const((S5_WIDTH, S5_WIDTH)), const((1, S5_WIDTH))],
        out_specs=[pl.BlockSpec((tt * nb, S5_WIDTH), lambda t: (t, 0)), const((nb, 2 * S5_CH))],
        out_shape=[jax.ShapeDtypeStruct((rows, S5_WIDTH), F32), jax.ShapeDtypeStruct((nb, 2 * S5_CH), F32)],
        scratch_shapes=[pltpu.VMEM((tt * nb, 2 * S5_CH), F32), pltpu.VMEM((nb, 2 * S5_CH), F32)],
        compiler_params=_cparams(("arbitrary",)),
        name="s5_scan",
    )(u_tm, bb, ab, h0, cc, d, wglu, bglu)


def _split3(x):
    hi = x.astype(BF16)
    r = x - hi.astype(F32)
    mid = r.astype(BF16)
    lo = (r - mid.astype(F32)).astype(BF16)
    return hi, mid, lo


def _gla_prompt_kernel(gq_ref, gk_ref, gv_ref, gr_ref, lf_ref, gn_ref, o_ref, sfin_ref, st_ref, *, nchunk):
    c = GLA_CHUNK

    @pl.when(pl.program_id(1) == 0)
    def _():
        st_ref[...] = jnp.zeros_like(st_ref)

    row = lax.broadcasted_iota(jnp.int32, (c, c), 0)
    col = lax.broadcasted_iota(jnp.int32, (c, c), 1)
    causal = col <= row
    tri = causal.astype(BF16)
    gn = gn_ref[...]

    for ci in range(nchunk):
        rows = slice(ci * c, (ci + 1) * c)
        g_hi, g_mid, g_lo = _split3(lf_ref[rows, :])
        bcum = _mm(tri, g_hi) + _mm(tri, g_mid) + _mm(tri, g_lo)
        blast = bcum[c - 1:c, :]
        qd = (gq_ref[rows, :] * jnp.exp(bcum)).astype(BF16)
        kd = (gk_ref[rows, :] * jnp.exp(-bcum)).astype(BF16)
        kt = (gk_ref[rows, :] * jnp.exp(blast - bcum)).astype(BF16)
        st = st_ref[...]
        st_ref[...] = st * jnp.exp(blast)
        st16 = st.astype(BF16)
        for hd in range(GLA_HEADS):
            ks = slice(hd * GLA_DK, (hd + 1) * GLA_DK)
            vs = slice(hd * GLA_DV, (hd + 1) * GLA_DV)
            vh = gv_ref[rows, vs].astype(BF16)
            att = jnp.where(causal, _nt(qd[:, ks], kd[:, ks]), 0.0)
            o = _nt(qd[:, ks], st16[:, ks]) + _mm(att.astype(BF16), vh)
            st_ref[:, ks] += _tn(vh, kt[:, ks])
            o_ref[rows, vs] = _rms(o, gn) * _silu(gr_ref[rows, vs])

    sfin_ref[...] = st_ref[...]


def _gla_prompt(gq, gk, gv, gr, lf, gnorm, nseq, seqlen):
    assert seqlen % GLA_CHUNK == 0
    tt = min(256, seqlen)
    nt = seqlen // tt
    tok = lambda width: pl.BlockSpec((tt, width), lambda b, t: (b * nt + t, 0))
    kern = functools.partial(_gla_prompt_kernel, nchunk=tt // GLA_CHUNK)
    return pl.pallas_call(
        kern,
        grid=(nseq, nt),
        in_specs=[tok(GLA_KEY_WIDTH), tok(GLA_KEY_WIDTH), tok(GLA_VAL_WIDTH), tok(GLA_VAL_WIDTH),
                  tok(GLA_KEY_WIDTH), pl.BlockSpec((1, GLA_DV), lambda b, t: (0, 0))],
        out_specs=[tok(GLA_VAL_WIDTH),
                   pl.BlockSpec((None, GLA_DV, GLA_KEY_WIDTH), lambda b, t: (b, 0, 0))],
        out_shape=[jax.ShapeDtypeStruct((nseq * seqlen, GLA_VAL_WIDTH), F32),
                   jax.ShapeDtypeStruct((nseq, GLA_DV, GLA_KEY_WIDTH), F32)],
        scratch_shapes=[pltpu.VMEM((GLA_DV, GLA_KEY_WIDTH), F32)],
        compiler_params=_cparams(("parallel", "arbitrary")),
        name="gla_prompt",
    )(gq, gk, gv, gr, lf, gnorm)


def _gla_decode_kernel(q_ref, k_ref, g_ref, v_ref, r_ref, s0_ref, gn_ref, o_ref, s_ref):
    s = jnp.exp(g_ref[...]) * s0_ref[...] + k_ref[...] * v_ref[...]
    s_ref[...] = s
    o = jnp.sum(q_ref[...] * s, axis=2, keepdims=True)
    o_ref[...] = _rms(o, gn_ref[...]) * _silu(r_ref[...])


def _gla_decode(gq, gk, lf, gv, gr, s0, gnorm):
    n = gq.shape[0]
    col = lambda a: a.reshape(n, GLA_HEADS, GLA_DK, 1)
    rowv = lambda a: a.reshape(n, GLA_HEADS, 1, GLA_DV)
    g = 8 if n % 8 == 0 else 1
    cspec = pl.BlockSpec((g, GLA_HEADS, GLA_DK, 1), lambda b: (b, 0, 0, 0))
    rspec = pl.BlockSpec((g, GLA_HEADS, 1, GLA_DV), lambda b: (b, 0, 0, 0))
    sspec = pl.BlockSpec((g, GLA_HEADS, GLA_DK, GLA_DV), lambda b: (b, 0, 0, 0))
    o, s = pl.pallas_call(
        _gla_decode_kernel,
        grid=(n // g,),
        in_specs=[cspec, cspec, cspec, rspec, rspec, sspec,
                  pl.BlockSpec((1, 1, 1, GLA_DV), lambda b: (0, 0, 0, 0))],
        out_specs=[rspec, sspec],
        out_shape=[jax.ShapeDtypeStruct((n, GLA_HEADS, 1, GLA_DV), F32),
                   jax.ShapeDtypeStruct((n, GLA_HEADS, GLA_DK, GLA_DV), F32)],
        compiler_params=_cparams(("parallel",)),
        name="gla_decode",
    )(col(gq), col(gk), col(lf), rowv(gv), rowv(gr), s0, gnorm.reshape(1, 1, 1, GLA_DV))
    return o.reshape(n, GLA_VAL_WIDTH), s


def _out_ffn_kernel(x_ref, a_ref, b_ref, c_ref, wo_ref, nf_ref, wg_ref, wu_ref, wd_ref, nfin_ref,
                    o_ref, hf_ref, acc_ref, *, final_norm):
    j = pl.program_id(2)

    @pl.when(j == 0)
    def _():
        mix = jnp.concatenate([a_ref[...], b_ref[...], c_ref[...]], axis=1).astype(BF16)
        x1 = x_ref[...] + _mm(mix, wo_ref[...])
        acc_ref[...] = x1
        hf_ref[...] = _rms(x1, nf_ref[...]).astype(BF16)

    hf = hf_ref[...]
    act = _silu(_mm(hf, wg_ref[...])) * _mm(hf, wu_ref[...])
    acc_ref[...] += _mm(act.astype(BF16), wd_ref[...])

    @pl.when(j == pl.num_programs(2) - 1)
    def _():
        y = acc_ref[...]
        o_ref[...] = _rms(y, nfin_ref[...]) if final_norm else y


def _out_ffn(x2d, a, b_tm, c, nseq, seqlen, w_out, norm_ffn, w_gate, w_up, w_down, norm_final, final_norm):
    tm = min(512, seqlen)
    nt = seqlen // tm
    th = FFN_HIDDEN // 2
    nh = FFN_HIDDEN // th
    tok = lambda width: pl.BlockSpec((tm, width), lambda b, t, j: (b * nt + t, 0))
    const = lambda shape: pl.BlockSpec(shape, lambda b, t, j: (0,) * len(shape))
    kern = functools.partial(_out_ffn_kernel, final_norm=final_norm)
    return pl.pallas_call(
        kern,
        grid=(nseq, nt, nh),
        in_specs=[tok(D_MODEL), tok(MOBA_WIDTH),
                  pl.BlockSpec((tm, S5_WIDTH), lambda b, t, j: (t, b)),
                  tok(GLA_VAL_WIDTH), const((D_MODEL, D_MODEL)), const((1, D_MODEL)),
                  pl.BlockSpec((D_MODEL, th), lambda b, t, j: (0, j)),
                  pl.BlockSpec((D_MODEL, th), lambda b, t, j: (0, j)),
                  pl.BlockSpec((th, D_MODEL), lambda b, t, j: (j, 0)),
                  const((1, D_MODEL))],
        out_specs=tok(D_MODEL),
        out_shape=jax.ShapeDtypeStruct((nseq * seqlen, D_MODEL), F32),
        scratch_shapes=[pltpu.VMEM((tm, D_MODEL), BF16), pltpu.VMEM((tm, D_MODEL), F32)],
        compiler_params=_cparams(("parallel", "parallel", "arbitrary")),
        name="out_ffn",
    )(x2d, a, b_tm, c, w_out, norm_ffn, w_gate, w_up, w_down, norm_final)


def _prep_weights(w_in, w_out, s5_w_glu, gla_w_gate, w_ffn_gate, w_ffn_up, w_ffn_down):
    depth = w_in.shape[0]
    w_in_p = jnp.pad(w_in, ((0, 0), (0, 0), (0, IN_COLS_PAD - IN_COLS))).astype(BF16)
    wg_p = jnp.pad(gla_w_gate, ((0, 0), (0, LANE - GLA_GATE_RANK), (0, 0))).astype(BF16)
    del depth
    return (w_in_p, w_out.astype(BF16), s5_w_glu.astype(BF16), wg_p,
            w_ffn_gate.astype(BF16), w_ffn_up.astype(BF16), w_ffn_down.astype(BF16))


def _trunk(x, kv_pool, ssm_re0, ssm_im0, gla0, params):
    (norm_mix, w_in_p, w_out, ab, bb, cc, s5_d, w_glu, s5_b_glu, wg_p, gla_b_gate, gla_norm,
     norm_ffn, w_gate, w_up, w_down, norm_final) = params
    nseq, seqlen, _ = x.shape
    depth = w_in_p.shape[0]
    decode = kv_pool is not None
    if decode:
        assert seqlen == 1
        cache_k, cache_v, page_table = kv_pool
        cache_kt = jnp.transpose(cache_k, (0, 1, 3, 4, 2))
        cache_vt = jnp.transpose(cache_v, (0, 1, 3, 4, 2))
        tw_nseq, tw_len = 1, nseq
    else:
        assert seqlen % MOBA_BLOCK == 0
        tw_nseq, tw_len = nseq, seqlen
    n = nseq * seqlen
    x2d = x.reshape(n, D_MODEL)
    k_rows, v_rows, s_re, s_im, s_gla = [], [], [], [], []
    for l in range(depth):
        k, v, *q_parts, u_tm, gq, gk, gv, gr, lf = _in_proj(
            x2d, tw_nseq, tw_len, norm_mix[l][None], w_in_p[l], wg_p[l], gla_b_gate[l][None],
            head_major=not decode)
        k_rows.append(k)
        v_rows.append(v)
        if decode:
            heads = lambda a: a.reshape(nseq, MOBA_HEADS, MOBA_HEAD_DIM)
            a_out = _moba_decode(page_table, heads(q_parts[0]), heads(k), heads(v), cache_kt, cache_vt, l)
        else:
            a_out = _moba_prompt(*q_parts)
        if ssm_re0 is None:
            h0 = jnp.zeros((nseq, 2 * S5_CH), F32)
        else:
            h0 = jnp.concatenate([ssm_re0[l].reshape(nseq, S5_CH), ssm_im0[l].reshape(nseq, S5_CH)], axis=1)
        b_tm, h_fin = _s5(u_tm.reshape(n, S5_WIDTH), nseq, bb[l], ab[l], h0, cc[l], s5_d[l][None],
                          w_glu[l], s5_b_glu[l][None])
        s_re.append(h_fin[:, :S5_CH].reshape(nseq, S5_GROUPS, S5_STATE))
        s_im.append(h_fin[:, S5_CH:].reshape(nseq, S5_GROUPS, S5_STATE))
        b_tm = b_tm.reshape(tw_len, tw_nseq * S5_WIDTH)
        if decode:
            c_out, s_fin = _gla_decode(gq, gk, lf, gv, gr, gla0[l], gla_norm[l])
        else:
            c_out, s_t = _gla_prompt(gq, gk, gv, gr, lf, gla_norm[l][None], nseq, seqlen)
            s_fin = jnp.transpose(s_t.reshape(nseq, GLA_DV, GLA_HEADS, GLA_DK), (0, 2, 3, 1))
        s_gla.append(s_fin)
        x2d = _out_ffn(x2d, a_out, b_tm, c_out, tw_nseq, tw_len, w_out[l], norm_ffn[l][None],
                       w_gate[l], w_up[l], w_down[l], norm_final[None], final_norm=(l == depth - 1))
    if decode:
        shp = (nseq, seqlen, MOBA_HEADS, MOBA_HEAD_DIM)
        stack_kv = lambda rows: jnp.stack([r.reshape(shp) for r in rows], axis=1)
    else:
        shp = (nseq, MOBA_HEADS, MOBA_HEAD_DIM, seqlen)
        stack_kv = lambda rows: jnp.transpose(jnp.stack([r.reshape(shp) for r in rows], axis=1), (0, 1, 4, 2, 3))
    return (x2d.reshape(nseq, seqlen, D_MODEL), stack_kv(k_rows), stack_kv(v_rows),
            jnp.stack(s_re, axis=0), jnp.stack(s_im, axis=0), jnp.stack(s_gla, axis=0))


def kernel(x_prompt, x_sample, cache_k, cache_v, page_table, state_ssm_re, state_ssm_im, state_gla,
           norm_mix, w_in, w_out, s5_a_re, s5_a_im, s5_log_step, s5_b_re, s5_b_im, s5_c_re, s5_c_im,
           s5_d, s5_w_glu, s5_b_glu, gla_w_gate, gla_b_gate, gla_norm, norm_ffn, w_ffn_gate, w_ffn_up,
           w_ffn_down, norm_final):
    w_in_p, w_out16, w_glu16, wg_p, w_gate16, w_up16, w_down16 = _prep_weights(
        w_in, w_out, s5_w_glu, gla_w_gate, w_ffn_gate, w_ffn_up, w_ffn_down)
    ab, bb = _s5_discretise(s5_a_re, s5_a_im, s5_log_step, s5_b_re, s5_b_im)
    cc = _s5_out_map(s5_c_re, s5_c_im)
    params = (norm_mix, w_in_p, w_out16, ab, bb, cc, s5_d, w_glu16, s5_b_glu, wg_p, gla_b_gate, gla_norm,
              norm_ffn, w_gate16, w_up16, w_down16, norm_final)
    y_p, k_p, v_p, re_p, im_p, gla_p = _trunk(x_prompt, None, None, None, None, params)
    y_s, k_s, v_s, re_s, im_s, gla_s = _trunk(x_sample, (cache_k, cache_v, page_table),
                                              state_ssm_re, state_ssm_im, state_gla, params)
    return (y_p, y_s, k_p, v_p, k_s, v_s, re_p, im_p, re_s, im_s, gla_p, gla_s)
```

```python
import functools
import math

import jax
import jax.numpy as jnp
from jax import lax
from jax.experimental import pallas as pl
from jax.experimental.pallas import tpu as pltpu

F32 = jnp.float32
BF16 = jnp.bfloat16

D_MODEL = 1024
PAGE_SIZE = 128
MOBA_HEAD_DIM = 64
MOBA_HEADS = 8
MOBA_WIDTH = MOBA_HEADS * MOBA_HEAD_DIM
MOBA_BLOCK = 256
MOBA_TOPK = 3
MOBA_VROWS = 80
LOG2E = 1.4426950408889634
S5_WIDTH = 256
S5_GROUP = 16
S5_GROUPS = 16
S5_STATE = 64
S5_CH = S5_GROUPS * S5_STATE
GLA_HEADS = 4
GLA_DK = 32
GLA_DV = 64
GLA_KEY_WIDTH = GLA_HEADS * GLA_DK
GLA_VAL_WIDTH = GLA_HEADS * GLA_DV
GLA_GATE_RANK = 16
GLA_GATE_NORM = 16.0
GLA_CHUNK = 64
FFN_HIDDEN = 2816
RMS_EPS = 1e-6
NEG_INF = float("-inf")

_OFF_Q = 0
_OFF_K = _OFF_Q + MOBA_WIDTH
_OFF_V = _OFF_K + MOBA_WIDTH
_OFF_U = _OFF_V + MOBA_WIDTH
_OFF_GQ = _OFF_U + S5_WIDTH
_OFF_GK = _OFF_GQ + GLA_KEY_WIDTH
_OFF_GV = _OFF_GK + GLA_KEY_WIDTH
_OFF_GR = _OFF_GV + GLA_VAL_WIDTH
_OFF_GLR = _OFF_GR + GLA_VAL_WIDTH
IN_COLS = _OFF_GLR + GLA_GATE_RANK
LANE = 128
IN_COLS_PAD = _OFF_GLR + LANE

VMEM_LIMIT = 56 * 1024 * 1024


def _cparams(sem):
    return pltpu.CompilerParams(dimension_semantics=sem, vmem_limit_bytes=VMEM_LIMIT)


def _nt(a, b):
    return lax.dot_general(a, b, (((1,), (1,)), ((), ())), preferred_element_type=F32)


def _tn(a, b):
    return lax.dot_general(a, b, (((0,), (0,)), ((), ())), preferred_element_type=F32)


def _mm(a, b):
    return jnp.dot(a, b, preferred_element_type=F32)


def _sigmoid(x):
    return 1.0 / (1.0 + jnp.exp(-x))


def _silu(x):
    return x * _sigmoid(x)


def _gelu_tanh(x):
    c = math.sqrt(2.0 / math.pi)
    return 0.5 * x * (1.0 + jnp.tanh(c * (x + 0.044715 * (x * x * x))))


def _log_sigmoid(x):
    return jnp.minimum(x, 0.0) - jnp.log(1.0 + jnp.exp(-jnp.abs(x)))


def _rms(x, g):
    return x * lax.rsqrt(jnp.mean(x * x, axis=-1, keepdims=True) + RMS_EPS) * g


def _in_proj_kernel(x_ref, g_ref, w_ref, wg_ref, bg_ref, *rest, head_major, layer):
    if head_major and layer > 0:
        rest = rest[2:]
    k_ref, v_ref = rest[:2]
    rest = rest[2:]
    h = _rms(x_ref[...], g_ref[...]).astype(BF16)

    def proj(lo, width):
        return _mm(h, w_ref[:, lo:lo + width])

    q = proj(_OFF_Q, MOBA_WIDTH)
    k = proj(_OFF_K, MOBA_WIDTH)
    v = proj(_OFF_V, MOBA_WIDTH)
    if head_major:
        vt = v.T
        if layer == 0:
            k_ref[0] = k.T
            v_ref[0] = vt
            for later in range(1, k_ref.shape[0]):
                k_ref[later] = jnp.zeros(vt.shape, F32)
                v_ref[later] = jnp.zeros(vt.shape, F32)
        else:
            k_ref[...] = k.T
            v_ref[...] = vt
        qh_ref, kh_ref, vt_ref = rest[:3]
        rest = rest[3:]
        q16 = (q * (MOBA_HEAD_DIM ** -0.5 * LOG2E)).astype(BF16)
        k16 = k.astype(BF16)
        vt16 = vt.astype(BF16)
        for hd in range(MOBA_HEADS):
            sl = slice(hd * MOBA_HEAD_DIM, (hd + 1) * MOBA_HEAD_DIM)
            qh_ref[hd] = q16[:, sl]
            kh_ref[hd] = k16[:, sl]
            vt_ref[hd, :MOBA_HEAD_DIM, :] = vt16[sl, :]
            vt_ref[hd, MOBA_HEAD_DIM:, :] = jnp.ones((MOBA_VROWS - MOBA_HEAD_DIM, vt16.shape[1]), BF16)
    else:
        k_ref[...] = k
        v_ref[...] = v
        rest[0][...] = q
        rest = rest[1:]
    u_ref, gq_ref, gk_ref, gv_ref, gr_ref, lf_ref = rest
    u_ref[...] = proj(_OFF_U, S5_WIDTH)
    gq_ref[...] = proj(_OFF_GQ, GLA_KEY_WIDTH) * (GLA_DK ** -0.5)
    gk_ref[...] = proj(_OFF_GK, GLA_KEY_WIDTH)
    gv_ref[...] = proj(_OFF_GV, GLA_VAL_WIDTH)
    gr_ref[...] = proj(_OFF_GR, GLA_VAL_WIDTH)
    glr = proj(_OFF_GLR, LANE)
    gate = _mm(glr.astype(BF16), wg_ref[...]) + bg_ref[...]
    lf_ref[...] = _log_sigmoid(gate) * (1.0 / GLA_GATE_NORM)


def _in_proj(x2d, nseq, seqlen, norm_g, w_pad, wg_pad, bg, head_major, layer=0, depth=1, kv_stack=None):
    n = nseq * seqlen
    tm = min(512, seqlen)
    nt = seqlen // tm
    tok = lambda width: pl.BlockSpec((tm, width), lambda b, t: (b * nt + t, 0))
    const = lambda shape: pl.BlockSpec(shape, lambda b, t: (0,) * len(shape))
    tokf = lambda width: jax.ShapeDtypeStruct((n, width), F32)
    extra_in, extra_specs, aliases = [], [], {}
    if head_major:
        head = pl.BlockSpec((None, MOBA_HEADS, tm, MOBA_HEAD_DIM), lambda b, t: (b, 0, t, 0))
        head_t = pl.BlockSpec((None, MOBA_HEADS, MOBA_VROWS, tm), lambda b, t: (b, 0, 0, t))
        if layer == 0:
            kv_t = pl.BlockSpec((None, depth, MOBA_WIDTH, tm), lambda b, t: (b, 0, 0, t))
        else:
            kv_t = pl.BlockSpec((None, None, MOBA_WIDTH, tm), lambda b, t: (b, layer, 0, t))
            extra_in = list(kv_stack)
            extra_specs = [pl.BlockSpec(memory_space=pl.ANY)] * 2
            aliases = {5: 0, 6: 1}
        q_specs = [kv_t, kv_t, head, head, head_t]
        q_shapes = [jax.ShapeDtypeStruct((nseq, depth, MOBA_WIDTH, seqlen), F32)] * 2 + [
            jax.ShapeDtypeStruct((nseq, MOBA_HEADS, seqlen, MOBA_HEAD_DIM), BF16)] * 2 + [
            jax.ShapeDtypeStruct((nseq, MOBA_HEADS, MOBA_VROWS, seqlen), BF16)]
    else:
        q_specs = [tok(MOBA_WIDTH), tok(MOBA_WIDTH), tok(MOBA_WIDTH)]
        q_shapes = [tokf(MOBA_WIDTH)] * 3
    return pl.pallas_call(
        functools.partial(_in_proj_kernel, head_major=head_major, layer=layer),
        grid=(nseq, nt),
        in_specs=[tok(D_MODEL), const((1, D_MODEL)), const((D_MODEL, IN_COLS_PAD)),
                  const((LANE, GLA_KEY_WIDTH)), const((1, GLA_KEY_WIDTH))] + extra_specs,
        input_output_aliases=aliases,
        out_specs=q_specs + [
            pl.BlockSpec((tm, S5_WIDTH), lambda b, t: (t, b)),
            tok(GLA_KEY_WIDTH), tok(GLA_KEY_WIDTH), tok(GLA_VAL_WIDTH), tok(GLA_VAL_WIDTH),
            tok(GLA_KEY_WIDTH)],
        out_shape=q_shapes + [
            jax.ShapeDtypeStruct((seqlen, nseq * S5_WIDTH), F32),
            tokf(GLA_KEY_WIDTH), tokf(GLA_KEY_WIDTH), tokf(GLA_VAL_WIDTH), tokf(GLA_VAL_WIDTH),
            tokf(GLA_KEY_WIDTH)],
        compiler_params=_cparams(("parallel", "parallel")),
        name="in_proj",
    )(x2d, norm_g, w_pad, wg_pad, bg, *extra_in)


MOBA_STEP_HEADS = 8


def _moba_prompt_kernel(q_ref, k_ref, vt_ref, o_ref, means_ref, bias_ref, m_ref, acc_ref, alpha_ref, tmax_ref,
                        s_ref, p_ref, *, nblk, nblk_pad):
    i = pl.program_id(2)
    bs = MOBA_BLOCK
    heads = range(MOBA_STEP_HEADS)
    tile = 2 * bs

    @pl.when(i == 0)
    def _():
        means_ref[...] = jnp.zeros_like(means_ref)
        for hh in heads:
            for j in range(nblk):
                kj = k_ref[hh, j * bs:(j + 1) * bs, :].astype(F32)
                means_ref[hh, j:j + 1, :] = jnp.mean(kj, axis=0, keepdims=True)

    blk_iota = lax.broadcasted_iota(jnp.int32, (nblk_pad, bs), 0)
    blk_f = blk_iota.astype(F32)
    key_iota = lax.broadcasted_iota(jnp.int32, (bs, bs), 0)
    qry_iota = lax.broadcasted_iota(jnp.int32, (bs, bs), 1)
    causal_bias = jnp.where(key_iota <= qry_iota, 0.0, NEG_INF)
    own_first = (i % 2) == 0
    t_own = i // 2
    r_own = pl.multiple_of(t_own * tile, tile)

    gates = [_nt(means_ref[hh].astype(BF16), q_ref[hh]) for hh in heads]
    own_scores = [_nt(k_ref[hh, pl.ds(r_own, tile), :], q_ref[hh]) for hh in heads]
    past = blk_iota < i
    for hh in heads:
        gate = jnp.where(past, gates[hh], NEG_INF)
        chosen = jnp.zeros(gate.shape, jnp.bool_)
        for _ in range(MOBA_TOPK):
            mx = jnp.max(gate, axis=0, keepdims=True)
            idx = jnp.min(jnp.where(gate == mx, blk_f, float(nblk_pad)), axis=0, keepdims=True)
            pick = blk_f == idx
            chosen = chosen | pick
            gate = jnp.where(pick, NEG_INF, gate)
        bias_ref[hh] = jnp.where(chosen & past, 0.0, NEG_INF)

    def scores(t, slot):
        c0 = t * tile if isinstance(t, int) else pl.multiple_of(t * tile, tile)
        live = t < t_own
        for hh in heads:
            s = _nt(k_ref[hh, pl.ds(c0, tile), :], q_ref[hh])
            r0 = jnp.where(live, bias_ref[hh, pl.ds(2 * t, 1), :], NEG_INF)
            r1 = jnp.where(live, bias_ref[hh, pl.ds(2 * t + 1, 1), :], NEG_INF)
            s0 = s[:bs] + r0
            s1 = s[bs:] + r1
            s_ref[slot, hh, :bs, :] = s0
            s_ref[slot, hh, bs:, :] = s1
            tmax_ref[slot, hh] = jnp.max(jnp.maximum(s0, s1), axis=0, keepdims=True)

    def softmax(slot):
        for hh in heads:
            s = s_ref[slot, hh]
            m = m_ref[hh]
            m_new = jnp.maximum(m, tmax_ref[slot, hh])
            alpha_ref[slot, hh] = jnp.exp2(m - m_new)
            m_ref[hh] = m_new
            p_ref[slot, hh] = jnp.exp2((s - m_new).astype(BF16))

    def values(slot, v0):
        for hh in heads:
            pv = _mm(vt_ref[hh, :, pl.ds(v0, tile)], p_ref[slot, hh])
            acc_ref[hh] = alpha_ref[slot, hh] * acc_ref[hh] + pv

    scores(0, 0)
    for hh in heads:
        prev = jnp.broadcast_to(bias_ref[hh, pl.ds(2 * t_own, 1), :], (bs, bs))
        first = jnp.where(own_first, causal_bias, prev)
        second = jnp.where(own_first, NEG_INF, causal_bias)
        s = own_scores[hh] + jnp.concatenate([first, second], axis=0)
        m = jnp.max(s, axis=0, keepdims=True)
        m_ref[hh] = m
        p_ref[1, hh] = jnp.exp2((s - m).astype(BF16))
        alpha_ref[1, hh] = jnp.zeros((1, bs), F32)
        acc_ref[hh] = jnp.zeros((MOBA_VROWS, bs), F32)
    n_pair = (t_own + 1) // 2
    last_tile = nblk // 2 - 1

    def body(u, carry):
        pending = jnp.where(u == 0, r_own, (2 * u - 1) * tile)
        values(1, pl.multiple_of(pending, tile))
        scores(2 * u + 1, 1)
        softmax(0)
        values(0, pl.multiple_of(2 * u * tile, tile))
        scores(jnp.minimum(2 * u + 2, last_tile), 0)
        softmax(1)
        return carry

    lax.fori_loop(0, n_pair, body, 0)
    pending = jnp.where(n_pair == 0, r_own, (2 * n_pair - 1) * tile)
    values(1, pl.multiple_of(pending, tile))
    dh = MOBA_HEAD_DIM
    out = jnp.concatenate([acc_ref[hh, :dh, :] / acc_ref[hh, dh:dh + 1, :] for hh in heads], axis=0)
    o_ref[...] = out.T


def _moba_prompt(qh, kh, vt):
    nseq, _, seqlen, dh = qh.shape
    bs = MOBA_BLOCK
    assert seqlen % (2 * bs) == 0
    nblk = seqlen // bs
    nblk_pad = -(-nblk // 8) * 8
    nh = MOBA_STEP_HEADS
    kern = functools.partial(_moba_prompt_kernel, nblk=nblk, nblk_pad=nblk_pad)
    return pl.pallas_call(
        kern,
        grid=(nseq, MOBA_HEADS // nh, nblk),
        in_specs=[pl.BlockSpec((None, nh, bs, dh), lambda b, h, i: (b, h, i, 0)),
                  pl.BlockSpec((None, nh, seqlen, dh), lambda b, h, i: (b, h, 0, 0)),
                  pl.BlockSpec((None, nh, MOBA_VROWS, seqlen), lambda b, h, i: (b, h, 0, 0))],
        out_specs=pl.BlockSpec((bs, nh * dh), lambda b, h, i: (b * nblk + i, h)),
        out_shape=jax.ShapeDtypeStruct((nseq * seqlen, MOBA_WIDTH), F32),
        scratch_shapes=[pltpu.VMEM((nh, nblk_pad, dh), F32), pltpu.VMEM((nh, nblk_pad, bs), F32),
                        pltpu.VMEM((nh, 1, bs), F32),
                        pltpu.VMEM((nh, MOBA_VROWS, bs), F32), pltpu.VMEM((2, nh, 1, bs), F32),
                        pltpu.VMEM((2, nh, 1, bs), F32),
                        pltpu.VMEM((2, nh, 2 * bs, bs), F32), pltpu.VMEM((2, nh, 2 * bs, bs), BF16)],
        compiler_params=_cparams(("parallel", "parallel", "arbitrary")),
        name="moba_prompt",
    )(qh, kh, vt)


def _moba_gate_kernel(pt_ref, qbd_ref, *rest, ppg):
    del pt_ref
    k_refs = rest[:ppg]
    sel_ref, g_ref = rest[ppg:]
    s_id = pl.program_id(1)
    lane_blk = lax.broadcasted_iota(jnp.int32, (MOBA_HEADS, LANE), 1)

    @pl.when(s_id == 0)
    def _():
        g_ref[...] = jnp.full_like(g_ref, NEG_INF)

    qbd = qbd_ref[...].astype(BF16)
    g = g_ref[...]
    for bi in range(ppg // 2):
        s = jnp.zeros((MOBA_HEADS, PAGE_SIZE), F32)
        for pg in range(2):
            kt = k_refs[2 * bi + pg][...].reshape(MOBA_WIDTH, PAGE_SIZE)
            s = s + _mm(qbd, kt.astype(BF16))
        gate = jnp.sum(s, axis=1, keepdims=True)
        g = jnp.where(lane_blk == s_id * (ppg // 2) + bi, gate, g)
    g_ref[...] = g

    @pl.when(s_id == pl.num_programs(1) - 1)
    def _():
        gg = g_ref[...]
        lane_f = lane_blk.astype(F32)
        picked = jnp.zeros((MOBA_HEADS, LANE), F32)
        for r in range(MOBA_TOPK):
            mx = jnp.max(gg, axis=1, keepdims=True)
            idx = jnp.min(jnp.where(gg == mx, lane_f, float(LANE)), axis=1, keepdims=True)
            picked = jnp.where(lane_blk == r, idx, picked)
            gg = jnp.where(lane_f == idx, NEG_INF, gg)
        sel_ref[...] = picked.astype(jnp.int32)


PICK_HEADS = 4


def _moba_pick_kernel(pt_ref, sel_ref, q_ref, kn_ref, vn_ref, *rest):
    del pt_ref, sel_ref
    npg = 2 * MOBA_TOPK
    k_refs = rest[:PICK_HEADS * npg]
    v_refs = rest[PICK_HEADS * npg:2 * PICK_HEADS * npg]
    o_ref = rest[2 * PICK_HEADS * npg]
    scale = MOBA_HEAD_DIM ** -0.5
    sub = 8

    for hs in range(PICK_HEADS):
        q = q_ref[hs]
        q8 = jnp.broadcast_to((q * scale).astype(BF16), (sub, MOBA_HEAD_DIM))
        s_own = jnp.sum(q * kn_ref[hs], axis=1, keepdims=True) * scale
        s = [_mm(q8, k_refs[hs * npg + j][...].astype(BF16))[0:1, :] for j in range(npg)]
        m = s_own
        for sj in s:
            m = jnp.maximum(m, jnp.max(sj, axis=1, keepdims=True))
        p_own = jnp.exp(s_own - m)
        den = p_own
        num = p_own * vn_ref[hs]
        for j in range(npg):
            p = jnp.exp(s[j] - m)
            den = den + jnp.sum(p, axis=1, keepdims=True)
            p8 = jnp.broadcast_to(p.astype(BF16), (sub, PAGE_SIZE))
            num = num + _nt(p8, v_refs[hs * npg + j][...].astype(BF16))[0:1, :]
        o_ref[hs] = num / den


def _moba_decode(page_table, q, k_new, v_new, cache_kt, cache_vt, layer):
    nseq, n_pages = page_table.shape
    nblk = n_pages * PAGE_SIZE // MOBA_BLOCK
    assert n_pages * PAGE_SIZE == nblk * MOBA_BLOCK and MOBA_TOPK <= nblk <= LANE
    ppg = 16 if n_pages % 16 == 0 else 2
    heads, dh = MOBA_HEADS, MOBA_HEAD_DIM

    qbd = (q[:, :, None, :] * jnp.eye(heads, dtype=F32)[None, :, :, None]).reshape(nseq, heads, heads * dh)

    def all_heads(p):
        return pl.BlockSpec((None, None, heads, dh, PAGE_SIZE),
                            lambda b, s, pt, p=p: (pt[b, s * ppg + p], layer, 0, 0, 0))

    sel = pl.pallas_call(
        functools.partial(_moba_gate_kernel, ppg=ppg),
        grid_spec=pltpu.PrefetchScalarGridSpec(
            num_scalar_prefetch=1,
            grid=(nseq, n_pages // ppg),
            in_specs=[pl.BlockSpec((None, heads, heads * dh), lambda b, s, pt: (b, 0, 0))]
            + [all_heads(p) for p in range(ppg)],
            out_specs=pl.BlockSpec((None, heads, LANE), lambda b, s, pt: (b, 0, 0)),
            scratch_shapes=[pltpu.VMEM((heads, LANE), F32)],
        ),
        out_shape=jax.ShapeDtypeStruct((nseq, heads, LANE), jnp.int32),
        compiler_params=_cparams(("parallel", "arbitrary")),
        name="moba_gate",
    )(page_table, qbd, *([cache_kt] * ppg))
    sel_flat = sel[:, :, :MOBA_TOPK].reshape(nseq * heads * MOBA_TOPK)

    def one_head(hs, j):
        def index(b, g, pt, sl):
            h = g * PICK_HEADS + hs
            blk = sl[(b * heads + h) * MOBA_TOPK + j // 2]
            return (pt[b, 2 * blk + j % 2], layer, h, 0, 0)
        return pl.BlockSpec((None, None, None, dh, PAGE_SIZE), index)

    row = pl.BlockSpec((None, PICK_HEADS, 1, dh), lambda b, g, pt, sl: (b, g, 0, 0))
    rows = lambda a: a.reshape(nseq, heads, 1, dh)
    pages = [one_head(hs, j) for hs in range(PICK_HEADS) for j in range(2 * MOBA_TOPK)]
    out = pl.pallas_call(
        _moba_pick_kernel,
        grid_spec=pltpu.PrefetchScalarGridSpec(
            num_scalar_prefetch=2,
            grid=(nseq, heads // PICK_HEADS),
            in_specs=[row, row, row] + pages + pages,
            out_specs=row,
        ),
        out_shape=jax.ShapeDtypeStruct((nseq, heads, 1, dh), F32),
        compiler_params=_cparams(("parallel", "parallel")),
        name="moba_pick",
    )(page_table, sel_flat, rows(q), rows(k_new), rows(v_new),
      *([cache_kt] * len(pages)), *([cache_vt] * len(pages)))
    return out.reshape(nseq, heads * dh)


def _s5_disc_kernel(are_ref, aim_ref, ls_ref, bre_ref, bim_ref, abr_ref, abi_ref, bbr_ref, bbi_ref):
    a_re = are_ref[...]
    a_im = aim_ref[...]
    step = jnp.exp(ls_ref[...])
    mag = jnp.exp(a_re * step)
    ab_re = mag * jnp.cos(a_im * step)
    ab_im = mag * jnp.sin(a_im * step)
    den = a_re * a_re + a_im * a_im
    n_re = ab_re - 1.0
    f_re = (n_re * a_re + ab_im * a_im) / den
    f_im = (ab_im * a_re - n_re * a_im) / den
    b_re = bre_ref[...]
    b_im = bim_ref[...]
    abr_ref[...] = ab_re
    abi_ref[...] = ab_im
    bbr_ref[...] = f_re * b_re - f_im * b_im
    bbi_ref[...] = f_re * b_im + f_im * b_re


def _s5_discretise(a_re, a_im, log_step, b_re, b_im):
    depth = a_re.shape[0]
    rows = depth * S5_GROUPS * S5_GROUP
    rep = lambda a: jnp.broadcast_to(a[:, :, None, :], (depth, S5_GROUPS, S5_GROUP, S5_STATE)).reshape(rows, S5_STATE)
    ls = jnp.broadcast_to(log_step[:, :, None, None], (depth, S5_GROUPS, S5_GROUP, S5_STATE)).reshape(rows, S5_STATE)
    bt = lambda b: jnp.swapaxes(b, 2, 3).reshape(rows, S5_STATE)
    shp = jax.ShapeDtypeStruct((rows, S5_STATE), F32)
    abr, abi, bbr, bbi = pl.pallas_call(
        _s5_disc_kernel, out_shape=[shp] * 4, name="s5_disc",
    )(rep(a_re), rep(a_im), ls, bt(b_re), bt(b_im))
    r4 = lambda a: a.reshape(depth, S5_GROUPS, S5_GROUP, S5_STATE)
    ab = jnp.concatenate([r4(abr)[:, :, 0].reshape(depth, 1, S5_CH),
                          r4(abi)[:, :, 0].reshape(depth, 1, S5_CH)], axis=1)
    eye = jnp.eye(S5_GROUPS, dtype=F32)
    bd = lambda a: (r4(a)[:, :, :, None, :] * eye[None, :, None, :, None]).reshape(depth, S5_WIDTH, S5_CH)
    bb = jnp.concatenate([bd(bbr), bd(bbi)], axis=2)
    return ab, bb.astype(BF16)


def _s5_out_map(c_re, c_im):
    depth = c_re.shape[0]
    eye = jnp.eye(S5_GROUPS, dtype=F32)
    bd = lambda c: (jnp.swapaxes(c, 2, 3)[:, :, :, None, :] * eye[None, :, None, :, None]).reshape(depth, S5_CH, S5_WIDTH)
    return jnp.concatenate([bd(c_re), -bd(c_im)], axis=1).astype(BF16)


S5_SCAN_LANES = 512


def _s5_kernel(u_ref, bb_ref, ab_ref, h0_ref, cc_ref, d_ref, wglu_ref, bglu_ref,
               o_ref, hfin_ref, xs_ref, hc_ref, *, nb, tt):
    @pl.when(pl.program_id(0) == 0)
    def _():
        hc_ref[...] = h0_ref[...]

    u = u_ref[...]
    xs_ref[...] = _mm(u.astype(BF16), bb_ref[...])

    for c0 in range(0, S5_CH, S5_SCAN_LANES):
        re_sl = slice(c0, c0 + S5_SCAN_LANES)
        im_sl = slice(S5_CH + c0, S5_CH + c0 + S5_SCAN_LANES)
        ar = jnp.broadcast_to(ab_ref[0:1, re_sl], (nb, S5_SCAN_LANES))
        ai = jnp.broadcast_to(ab_ref[1:2, re_sl], (nb, S5_SCAN_LANES))

        def step(t, carry, re_sl=re_sl, im_sl=im_sl, ar=ar, ai=ai):
            hr, hi = carry
            rows = pl.ds(pl.multiple_of(t * nb, nb), nb)
            nhr = ar * hr - ai * hi + xs_ref[rows, re_sl]
            nhi = ar * hi + ai * hr + xs_ref[rows, im_sl]
            xs_ref[rows, re_sl] = nhr
            xs_ref[rows, im_sl] = nhi
            return nhr, nhi

        hr, hi = lax.fori_loop(0, tt, step, (hc_ref[:, re_sl], hc_ref[:, im_sl]), unroll=min(4, tt))
        hc_ref[:, re_sl] = hr
        hc_ref[:, im_sl] = hi

    y = _mm(xs_ref[...].astype(BF16), cc_ref[...]) + d_ref[...] * u
    yg = _gelu_tanh(y)
    gate = _sigmoid(_mm(yg.astype(BF16), wglu_ref[...]) + bglu_ref[...])
    o_ref[...] = yg * gate
    hfin_ref[...] = hc_ref[...]


def _s5(u_tm, nb, bb, ab, h0, cc, d, wglu, bglu):
    rows = u_tm.shape[0]
    t_total = rows // nb
    tt = min(64, t_total)
    const = lambda shape: pl.BlockSpec(shape, lambda t: (0,) * len(shape))
    kern = functools.partial(_s5_kernel, nb=nb, tt=tt)
    return pl.pallas_call(
        kern,
        grid=(t_total // tt,),
        in_specs=[pl.BlockSpec((tt * nb, S5_WIDTH), lambda t: (t, 0)),
                  const((S5_WIDTH, 2 * S5_CH)), const((2, S5_CH)), const((nb, 2 * S5_CH)),
                  const((2 * S5_CH, S5_WIDTH)), const((1, S5_WIDTH)),
                  const((S5_WIDTH, S5_WIDTH)), const((1, S5_WIDTH))],
        out_specs=[pl.BlockSpec((tt * nb, S5_WIDTH), lambda t: (t, 0)), const((nb, 2 * S5_CH))],
        out_shape=[jax.ShapeDtypeStruct((rows, S5_WIDTH), F32), jax.ShapeDtypeStruct((nb, 2 * S5_CH), F32)],
        scratch_shapes=[pltpu.VMEM((tt * nb, 2 * S5_CH), F32), pltpu.VMEM((nb, 2 * S5_CH), F32)],
        compiler_params=_cparams(("arbitrary",)),
        name="s5_scan",
    )(u_tm, bb, ab, h0, cc, d, wglu, bglu)


def _split3(x):
    hi = x.astype(BF16)
    r = x - hi.astype(F32)
    mid = r.astype(BF16)
    lo = (r - mid.astype(F32)).astype(BF16)
    return hi, mid, lo


def _gla_prompt_kernel(gq_ref, gk_ref, gv_ref, gr_ref, lf_ref, gn_ref, o_ref, sfin_ref, st_ref, *, nchunk):
    c = GLA_CHUNK

    @pl.when(pl.program_id(1) == 0)
    def _():
        st_ref[...] = jnp.zeros_like(st_ref)

    row = lax.broadcasted_iota(jnp.int32, (c, c), 0)
    col = lax.broadcasted_iota(jnp.int32, (c, c), 1)
    causal = col <= row
    tri = causal.astype(BF16)
    gn = gn_ref[...]

    for ci in range(nchunk):
        rows = slice(ci * c, (ci + 1) * c)
        g_hi, g_mid, g_lo = _split3(lf_ref[rows, :])
        bcum = _mm(tri, g_hi) + _mm(tri, g_mid) + _mm(tri, g_lo)
        blast = bcum[c - 1:c, :]
        qd = (gq_ref[rows, :] * jnp.exp(bcum)).astype(BF16)
        kd = (gk_ref[rows, :] * jnp.exp(-bcum)).astype(BF16)
        kt = (gk_ref[rows, :] * jnp.exp(blast - bcum)).astype(BF16)
        st = st_ref[...]
        st_ref[...] = st * jnp.exp(blast)
        st16 = st.astype(BF16)
        for hd in range(GLA_HEADS):
            ks = slice(hd * GLA_DK, (hd + 1) * GLA_DK)
            vs = slice(hd * GLA_DV, (hd + 1) * GLA_DV)
            vh = gv_ref[rows, vs].astype(BF16)
            att = jnp.where(causal, _nt(qd[:, ks], kd[:, ks]), 0.0)
            o = _nt(qd[:, ks], st16[:, ks]) + _mm(att.astype(BF16), vh)
            st_ref[:, ks] += _tn(vh, kt[:, ks])
            o_ref[rows, vs] = _rms(o, gn) * _silu(gr_ref[rows, vs])

    sfin_ref[...] = st_ref[...]


def _gla_prompt(gq, gk, gv, gr, lf, gnorm, nseq, seqlen):
    assert seqlen % GLA_CHUNK == 0
    tt = min(256, seqlen)
    nt = seqlen // tt
    tok = lambda width: pl.BlockSpec((tt, width), lambda b, t: (b * nt + t, 0))
    kern = functools.partial(_gla_prompt_kernel, nchunk=tt // GLA_CHUNK)
    return pl.pallas_call(
        kern,
        grid=(nseq, nt),
        in_specs=[tok(GLA_KEY_WIDTH), tok(GLA_KEY_WIDTH), tok(GLA_VAL_WIDTH), tok(GLA_VAL_WIDTH),
                  tok(GLA_KEY_WIDTH), pl.BlockSpec((1, GLA_DV), lambda b, t: (0, 0))],
        out_specs=[tok(GLA_VAL_WIDTH),
                   pl.BlockSpec((None, GLA_DV, GLA_KEY_WIDTH), lambda b, t: (b, 0, 0))],
        out_shape=[jax.ShapeDtypeStruct((nseq * seqlen, GLA_VAL_WIDTH), F32),
                   jax.ShapeDtypeStruct((nseq, GLA_DV, GLA_KEY_WIDTH), F32)],
        scratch_shapes=[pltpu.VMEM((GLA_DV, GLA_KEY_WIDTH), F32)],
        compiler_params=_cparams(("parallel", "arbitrary")),
        name="gla_prompt",
    )(gq, gk, gv, gr, lf, gnorm)


def _gla_decode_kernel(q_ref, k_ref, g_ref, v_ref, r_ref, s0_ref, gn_ref, o_ref, s_ref):
    s = jnp.exp(g_ref[...]) * s0_ref[...] + k_ref[...] * v_ref[...]
    s_ref[...] = s
    o = jnp.sum(q_ref[...] * s, axis=2, keepdims=True)
    o_ref[...] = _rms(o, gn_ref[...]) * _silu(r_ref[...])


def _gla_decode(gq, gk, lf, gv, gr, s0, gnorm):
    n = gq.shape[0]
    col = lambda a: a.reshape(n, GLA_HEADS, GLA_DK, 1)
    rowv = lambda a: a.reshape(n, GLA_HEADS, 1, GLA_DV)
    g = 8 if n % 8 == 0 else 1
    cspec = pl.BlockSpec((g, GLA_HEADS, GLA_DK, 1), lambda b: (b, 0, 0, 0))
    rspec = pl.BlockSpec((g, GLA_HEADS, 1, GLA_DV), lambda b: (b, 0, 0, 0))
    sspec = pl.BlockSpec((g, GLA_HEADS, GLA_DK, GLA_DV), lambda b: (b, 0, 0, 0))
    o, s = pl.pallas_call(
        _gla_decode_kernel,
        grid=(n // g,),
        in_specs=[cspec, cspec, cspec, rspec, rspec, sspec,
                  pl.BlockSpec((1, 1, 1, GLA_DV), lambda b: (0, 0, 0, 0))],
        out_specs=[rspec, sspec],
        out_shape=[jax.ShapeDtypeStruct((n, GLA_HEADS, 1, GLA_DV), F32),
                   jax.ShapeDtypeStruct((n, GLA_HEADS, GLA_DK, GLA_DV), F32)],
        compiler_params=_cparams(("parallel",)),
        name="gla_decode",
    )(col(gq), col(gk), col(lf), rowv(gv), rowv(gr), s0, gnorm.reshape(1, 1, 1, GLA_DV))
    return o.reshape(n, GLA_VAL_WIDTH), s


def _out_ffn_kernel(x_ref, a_ref, b_ref, c_ref, wo_ref, nf_ref, wg_ref, wu_ref, wd_ref, nfin_ref,
                    o_ref, hf_ref, acc_ref, *, final_norm):
    j = pl.program_id(2)

    @pl.when(j == 0)
    def _():
        mix = jnp.concatenate([a_ref[...], b_ref[...], c_ref[...]], axis=1).astype(BF16)
        x1 = x_ref[...] + _mm(mix, wo_ref[...])
        acc_ref[...] = x1
        hf_ref[...] = _rms(x1, nf_ref[...]).astype(BF16)

    hf = hf_ref[...]
    act = _silu(_mm(hf, wg_ref[...])) * _mm(hf, wu_ref[...])
    acc_ref[...] += _mm(act.astype(BF16), wd_ref[...])

    @pl.when(j == pl.num_programs(2) - 1)
    def _():
        y = acc_ref[...]
        o_ref[...] = _rms(y, nfin_ref[...]) if final_norm else y


def _out_ffn(x2d, a, b_tm, c, nseq, seqlen, w_out, norm_ffn, w_gate, w_up, w_down, norm_final, final_norm):
    tm = min(512, seqlen)
    nt = seqlen // tm
    th = FFN_HIDDEN // 2
    nh = FFN_HIDDEN // th
    tok = lambda width: pl.BlockSpec((tm, width), lambda b, t, j: (b * nt + t, 0))
    const = lambda shape: pl.BlockSpec(shape, lambda b, t, j: (0,) * len(shape))
    kern = functools.partial(_out_ffn_kernel, final_norm=final_norm)
    return pl.pallas_call(
        kern,
        grid=(nseq, nt, nh),
        in_specs=[tok(D_MODEL), tok(MOBA_WIDTH),
                  pl.BlockSpec((tm, S5_WIDTH), lambda b, t, j: (t, b)),
                  tok(GLA_VAL_WIDTH), const((D_MODEL, D_MODEL)), const((1, D_MODEL)),
                  pl.BlockSpec((D_MODEL, th), lambda b, t, j: (0, j)),
                  pl.BlockSpec((D_MODEL, th), lambda b, t, j: (0, j)),
                  pl.BlockSpec((th, D_MODEL), lambda b, t, j: (j, 0)),
                  const((1, D_MODEL))],
        out_specs=tok(D_MODEL),
        out_shape=jax.ShapeDtypeStruct((nseq * seqlen, D_MODEL), F32),
        scratch_shapes=[pltpu.VMEM((tm, D_MODEL), BF16), pltpu.VMEM((tm, D_MODEL), F32)],
        compiler_params=_cparams(("parallel", "parallel", "arbitrary")),
        name="out_ffn",
    )(x2d, a, b_tm, c, w_out, norm_ffn, w_gate, w_up, w_down, norm_final)


def _prep_weights(w_in, w_out, s5_w_glu, gla_w_gate, w_ffn_gate, w_ffn_up, w_ffn_down):
    depth = w_in.shape[0]
    w_in_p = jnp.pad(w_in, ((0, 0), (0, 0), (0, IN_COLS_PAD - IN_COLS))).astype(BF16)
    wg_p = jnp.pad(gla_w_gate, ((0, 0), (0, LANE - GLA_GATE_RANK), (0, 0))).astype(BF16)
    del depth
    return (w_in_p, w_out.astype(BF16), s5_w_glu.astype(BF16), wg_p,
            w_ffn_gate.astype(BF16), w_ffn_up.astype(BF16), w_ffn_down.astype(BF16))


def _trunk(x, kv_pool, ssm_re0, ssm_im0, gla0, params):
    (norm_mix, w_in_p, w_out, ab, bb, cc, s5_d, w_glu, s5_b_glu, wg_p, gla_b_gate, gla_norm,
     norm_ffn, w_gate, w_up, w_down, norm_final) = params
    nseq, seqlen, _ = x.shape
    depth = w_in_p.shape[0]
    decode = kv_pool is not None
    if decode:
        assert seqlen == 1
        cache_k, cache_v, page_table = kv_pool
        cache_kt = jnp.transpose(cache_k, (0, 1, 3, 4, 2))
        cache_vt = jnp.transpose(cache_v, (0, 1, 3, 4, 2))
        tw_nseq, tw_len = 1, nseq
    else:
        assert seqlen % MOBA_BLOCK == 0
        tw_nseq, tw_len = nseq, seqlen
    n = nseq * seqlen
    x2d = x.reshape(n, D_MODEL)
    k_rows, v_rows, s_re, s_im, s_gla = [], [], [], [], []
    for l in range(depth):
        k, v, *q_parts, u_tm, gq, gk, gv, gr, lf = _in_proj(
            x2d, tw_nseq, tw_len, norm_mix[l][None], w_in_p[l], wg_p[l], gla_b_gate[l][None],
            head_major=not decode, layer=l, depth=depth, kv_stack=(k_rows[-1], v_rows[-1]) if l else None)
        k_rows.append(k)
        v_rows.append(v)
        if decode:
            heads = lambda a: a.reshape(nseq, MOBA_HEADS, MOBA_HEAD_DIM)
            a_out = _moba_decode(page_table, heads(q_parts[0]), heads(k), heads(v), cache_kt, cache_vt, l)
        else:
            a_out = _moba_prompt(*q_parts)
        if ssm_re0 is None:
            h0 = jnp.zeros((nseq, 2 * S5_CH), F32)
        else:
            h0 = jnp.concatenate([ssm_re0[l].reshape(nseq, S5_CH), ssm_im0[l].reshape(nseq, S5_CH)], axis=1)
        b_tm, h_fin = _s5(u_tm.reshape(n, S5_WIDTH), nseq, bb[l], ab[l], h0, cc[l], s5_d[l][None],
                          w_glu[l], s5_b_glu[l][None])
        s_re.append(h_fin[:, :S5_CH].reshape(nseq, S5_GROUPS, S5_STATE))
        s_im.append(h_fin[:, S5_CH:].reshape(nseq, S5_GROUPS, S5_STATE))
        b_tm = b_tm.reshape(tw_len, tw_nseq * S5_WIDTH)
        if decode:
            c_out, s_fin = _gla_decode(gq, gk, lf, gv, gr, gla0[l], gla_norm[l])
        else:
            c_out, s_t = _gla_prompt(gq, gk, gv, gr, lf, gla_norm[l][None], nseq, seqlen)
            s_fin = jnp.transpose(s_t.reshape(nseq, GLA_DV, GLA_HEADS, GLA_DK), (0, 2, 3, 1))
        s_gla.append(s_fin)
        x2d = _out_ffn(x2d, a_out, b_tm, c_out, tw_nseq, tw_len, w_out[l], norm_ffn[l][None],
                       w_gate[l], w_up[l], w_down[l], norm_final[None], final_norm=(l == depth - 1))
    if decode:
        shp = (nseq, seqlen, MOBA_HEADS, MOBA_HEAD_DIM)
        stack_kv = lambda rows: jnp.stack([r.reshape(shp) for r in rows], axis=1)
    else:
        shp = (nseq, depth, MOBA_HEADS, MOBA_HEAD_DIM, seqlen)
        stack_kv = lambda rows: jnp.transpose(rows[-1].reshape(shp), (0, 1, 4, 2, 3))
    return (x2d.reshape(nseq, seqlen, D_MODEL), stack_kv(k_rows), stack_kv(v_rows),
            jnp.stack(s_re, axis=0), jnp.stack(s_im, axis=0), jnp.stack(s_gla, axis=0))


def kernel(x_prompt, x_sample, cache_k, cache_v, page_table, state_ssm_re, state_ssm_im, state_gla,
           norm_mix, w_in, w_out, s5_a_re, s5_a_im, s5_log_step, s5_b_re, s5_b_im, s5_c_re, s5_c_im,
           s5_d, s5_w_glu, s5_b_glu, gla_w_gate, gla_b_gate, gla_norm, norm_ffn, w_ffn_gate, w_ffn_up,
           w_ffn_down, norm_final):
    w_in_p, w_out16, w_glu16, wg_p, w_gate16, w_up16, w_down16 = _prep_weights(
        w_in, w_out, s5_w_glu, gla_w_gate, w_ffn_gate, w_ffn_up, w_ffn_down)
    ab, bb = _s5_discretise(s5_a_re, s5_a_im, s5_log_step, s5_b_re, s5_b_im)
    cc = _s5_out_map(s5_c_re, s5_c_im)
    params = (norm_mix, w_in_p, w_out16, ab, bb, cc, s5_d, w_glu16, s5_b_glu, wg_p, gla_b_gate, gla_norm,
              norm_ffn, w_gate16, w_up16, w_down16, norm_final)
    y_p, k_p, v_p, re_p, im_p, gla_p = _trunk(x_prompt, None, None, None, None, params)
    y_s, k_s, v_s, re_s, im_s, gla_s = _trunk(x_sample, (cache_k, cache_v, page_table),
                                              state_ssm_re, state_ssm_im, state_gla, params)
    return (y_p, y_s, k_p, v_p, k_s, v_s, re_p, im_p, re_s, im_s, gla_p, gla_s)
```

```python
import functools
import math

import jax
import jax.numpy as jnp
from jax import lax
from jax.experimental import pallas as pl
from jax.experimental.pallas import tpu as pltpu

F32 = jnp.float32
BF16 = jnp.bfloat16

D_MODEL = 1024
PAGE_SIZE = 128
MOBA_HEAD_DIM = 64
MOBA_HEADS = 8
MOBA_WIDTH = MOBA_HEADS * MOBA_HEAD_DIM
MOBA_BLOCK = 256
MOBA_TOPK = 3
MOBA_VROWS = 80
LOG2E = 1.4426950408889634
S5_WIDTH = 256
S5_GROUP = 16
S5_GROUPS = 16
S5_STATE = 64
S5_CH = S5_GROUPS * S5_STATE
GLA_HEADS = 4
GLA_DK = 32
GLA_DV = 64
GLA_KEY_WIDTH = GLA_HEADS * GLA_DK
GLA_VAL_WIDTH = GLA_HEADS * GLA_DV
GLA_GATE_RANK = 16
GLA_GATE_NORM = 16.0
GLA_CHUNK = 64
FFN_HIDDEN = 2816
RMS_EPS = 1e-6
NEG_INF = float("-inf")

_OFF_Q = 0
_OFF_K = _OFF_Q + MOBA_WIDTH
_OFF_V = _OFF_K + MOBA_WIDTH
_OFF_U = _OFF_V + MOBA_WIDTH
_OFF_GQ = _OFF_U + S5_WIDTH
_OFF_GK = _OFF_GQ + GLA_KEY_WIDTH
_OFF_GV = _OFF_GK + GLA_KEY_WIDTH
_OFF_GR = _OFF_GV + GLA_VAL_WIDTH
_OFF_GLR = _OFF_GR + GLA_VAL_WIDTH
IN_COLS = _OFF_GLR + GLA_GATE_RANK
LANE = 128
IN_COLS_PAD = _OFF_GLR + LANE

VMEM_LIMIT = 56 * 1024 * 1024


def _cparams(sem):
    return pltpu.CompilerParams(dimension_semantics=sem, vmem_limit_bytes=VMEM_LIMIT)


def _nt(a, b):
    return lax.dot_general(a, b, (((1,), (1,)), ((), ())), preferred_element_type=F32)


def _tn(a, b):
    return lax.dot_general(a, b, (((0,), (0,)), ((), ())), preferred_element_type=F32)


def _mm(a, b):
    return jnp.dot(a, b, preferred_element_type=F32)


def _sigmoid(x):
    return 1.0 / (1.0 + jnp.exp(-x))


def _silu(x):
    return x * _sigmoid(x)


def _gelu_tanh(x):
    c = math.sqrt(2.0 / math.pi)
    return 0.5 * x * (1.0 + jnp.tanh(c * (x + 0.044715 * (x * x * x))))


def _log_sigmoid(x):
    return jnp.minimum(x, 0.0) - jnp.log(1.0 + jnp.exp(-jnp.abs(x)))


def _rms(x, g):
    return x * lax.rsqrt(jnp.mean(x * x, axis=-1, keepdims=True) + RMS_EPS) * g


def _in_proj_kernel(x_ref, g_ref, w_ref, wg_ref, bg_ref, *rest, head_major, layer):
    if head_major and layer > 0:
        rest = rest[2:]
    k_ref, v_ref = rest[:2]
    rest = rest[2:]
    h = _rms(x_ref[...], g_ref[...]).astype(BF16)

    def proj(lo, width):
        return _mm(h, w_ref[:, lo:lo + width])

    q = proj(_OFF_Q, MOBA_WIDTH)
    k = proj(_OFF_K, MOBA_WIDTH)
    v = proj(_OFF_V, MOBA_WIDTH)
    if head_major:
        vt = v.T
        if layer == 0:
            k_ref[0] = k.T
            v_ref[0] = vt
            for later in range(1, k_ref.shape[0]):
                k_ref[later] = jnp.zeros(vt.shape, F32)
                v_ref[later] = jnp.zeros(vt.shape, F32)
        else:
            k_ref[...] = k.T
            v_ref[...] = vt
        qh_ref, kh_ref, vt_ref = rest[:3]
        rest = rest[3:]
        q16 = (q * (MOBA_HEAD_DIM ** -0.5 * LOG2E)).astype(BF16)
        k16 = k.astype(BF16)
        vt16 = vt.astype(BF16)
        for hd in range(MOBA_HEADS):
            sl = slice(hd * MOBA_HEAD_DIM, (hd + 1) * MOBA_HEAD_DIM)
            qh_ref[hd] = q16[:, sl]
            kh_ref[hd] = k16[:, sl]
            vt_ref[hd, :MOBA_HEAD_DIM, :] = vt16[sl, :]
            vt_ref[hd, MOBA_HEAD_DIM:, :] = jnp.ones((MOBA_VROWS - MOBA_HEAD_DIM, vt16.shape[1]), BF16)
    else:
        k_ref[...] = k
        v_ref[...] = v
        rest[0][...] = q
        rest = rest[1:]
    u_ref, gq_ref, gk_ref, gv_ref, gr_ref, lf_ref = rest
    u_ref[...] = proj(_OFF_U, S5_WIDTH)
    gq_ref[...] = proj(_OFF_GQ, GLA_KEY_WIDTH) * (GLA_DK ** -0.5)
    gk_ref[...] = proj(_OFF_GK, GLA_KEY_WIDTH)
    gv_ref[...] = proj(_OFF_GV, GLA_VAL_WIDTH)
    gr_ref[...] = proj(_OFF_GR, GLA_VAL_WIDTH)
    glr = proj(_OFF_GLR, LANE)
    gate = _mm(glr.astype(BF16), wg_ref[...]) + bg_ref[...]
    lf_ref[...] = _log_sigmoid(gate) * (1.0 / GLA_GATE_NORM)


def _in_proj(x2d, nseq, seqlen, norm_g, w_pad, wg_pad, bg, head_major, layer=0, depth=1, kv_stack=None):
    n = nseq * seqlen
    tm = min(512, seqlen)
    nt = seqlen // tm
    tok = lambda width: pl.BlockSpec((tm, width), lambda b, t: (b * nt + t, 0))
    const = lambda shape: pl.BlockSpec(shape, lambda b, t: (0,) * len(shape))
    tokf = lambda width: jax.ShapeDtypeStruct((n, width), F32)
    extra_in, extra_specs, aliases = [], [], {}
    if head_major:
        head = pl.BlockSpec((None, MOBA_HEADS, tm, MOBA_HEAD_DIM), lambda b, t: (b, 0, t, 0))
        head_t = pl.BlockSpec((None, MOBA_HEADS, MOBA_VROWS, tm), lambda b, t: (b, 0, 0, t))
        if layer == 0:
            kv_t = pl.BlockSpec((None, depth, MOBA_WIDTH, tm), lambda b, t: (b, 0, 0, t))
        else:
            kv_t = pl.BlockSpec((None, None, MOBA_WIDTH, tm), lambda b, t: (b, layer, 0, t))
            extra_in = list(kv_stack)
            extra_specs = [pl.BlockSpec(memory_space=pl.ANY)] * 2
            aliases = {5: 0, 6: 1}
        q_specs = [kv_t, kv_t, head, head, head_t]
        q_shapes = [jax.ShapeDtypeStruct((nseq, depth, MOBA_WIDTH, seqlen), F32)] * 2 + [
            jax.ShapeDtypeStruct((nseq, MOBA_HEADS, seqlen, MOBA_HEAD_DIM), BF16)] * 2 + [
            jax.ShapeDtypeStruct((nseq, MOBA_HEADS, MOBA_VROWS, seqlen), BF16)]
    else:
        q_specs = [tok(MOBA_WIDTH), tok(MOBA_WIDTH), tok(MOBA_WIDTH)]
        q_shapes = [tokf(MOBA_WIDTH)] * 3
    return pl.pallas_call(
        functools.partial(_in_proj_kernel, head_major=head_major, layer=layer),
        grid=(nseq, nt),
        in_specs=[tok(D_MODEL), const((1, D_MODEL)), const((D_MODEL, IN_COLS_PAD)),
                  const((LANE, GLA_KEY_WIDTH)), const((1, GLA_KEY_WIDTH))] + extra_specs,
        input_output_aliases=aliases,
        out_specs=q_specs + [
            pl.BlockSpec((tm, S5_WIDTH), lambda b, t: (t, b)),
            tok(GLA_KEY_WIDTH), tok(GLA_KEY_WIDTH), tok(GLA_VAL_WIDTH), tok(GLA_VAL_WIDTH),
            tok(GLA_KEY_WIDTH)],
        out_shape=q_shapes + [
            jax.ShapeDtypeStruct((seqlen, nseq * S5_WIDTH), F32),
            tokf(GLA_KEY_WIDTH), tokf(GLA_KEY_WIDTH), tokf(GLA_VAL_WIDTH), tokf(GLA_VAL_WIDTH),
            tokf(GLA_KEY_WIDTH)],
        compiler_params=_cparams(("parallel", "parallel")),
        name="in_proj",
    )(x2d, norm_g, w_pad, wg_pad, bg, *extra_in)


MOBA_STEP_HEADS = 8


def _moba_prompt_kernel(q_ref, k_ref, vt_ref, o_ref, means_ref, bias_ref, m_ref, acc_ref, alpha_ref, tmax_ref,
                        s_ref, p_ref, *, nblk, nblk_pad):
    i = pl.program_id(2)
    bs = MOBA_BLOCK
    heads = range(MOBA_STEP_HEADS)
    tile = 2 * bs

    @pl.when(i == 0)
    def _():
        means_ref[...] = jnp.zeros_like(means_ref)
        for hh in heads:
            for j in range(nblk):
                kj = k_ref[hh, j * bs:(j + 1) * bs, :].astype(F32)
                means_ref[hh, j:j + 1, :] = jnp.mean(kj, axis=0, keepdims=True)

    blk_iota = lax.broadcasted_iota(jnp.int32, (nblk_pad, bs), 0)
    blk_f = blk_iota.astype(F32)
    key_iota = lax.broadcasted_iota(jnp.int32, (bs, bs), 0)
    qry_iota = lax.broadcasted_iota(jnp.int32, (bs, bs), 1)
    causal_bias = jnp.where(key_iota <= qry_iota, 0.0, NEG_INF)
    own_first = (i % 2) == 0
    t_own = i // 2
    r_own = pl.multiple_of(t_own * tile, tile)

    gates = [_nt(means_ref[hh].astype(BF16), q_ref[hh]) for hh in heads]
    own_scores = [_nt(k_ref[hh, pl.ds(r_own, tile), :], q_ref[hh]) for hh in heads]
    past = blk_iota < i
    for hh in heads:
        gate = jnp.where(past, gates[hh], NEG_INF)
        chosen = jnp.zeros(gate.shape, jnp.bool_)
        for _ in range(MOBA_TOPK):
            mx = jnp.max(gate, axis=0, keepdims=True)
            idx = jnp.min(jnp.where(gate == mx, blk_f, float(nblk_pad)), axis=0, keepdims=True)
            pick = blk_f == idx
            chosen = chosen | pick
            gate = jnp.where(pick, NEG_INF, gate)
        bias_ref[hh] = jnp.where(chosen & past, 0.0, NEG_INF)

    def scores(t, slot):
        c0 = t * tile if isinstance(t, int) else pl.multiple_of(t * tile, tile)
        live = t < t_own
        for hh in heads:
            s = _nt(k_ref[hh, pl.ds(c0, tile), :], q_ref[hh])
            r0 = jnp.where(live, bias_ref[hh, pl.ds(2 * t, 1), :], NEG_INF)
            r1 = jnp.where(live, bias_ref[hh, pl.ds(2 * t + 1, 1), :], NEG_INF)
            s0 = s[:bs] + r0
            s1 = s[bs:] + r1
            s_ref[slot, hh, :bs, :] = s0
            s_ref[slot, hh, bs:, :] = s1
            tmax_ref[slot, hh] = jnp.max(jnp.maximum(s0, s1), axis=0, keepdims=True)

    def softmax(slot):
        for hh in heads:
            s = s_ref[slot, hh]
            m = m_ref[hh]
            m_new = jnp.maximum(m, tmax_ref[slot, hh])
            alpha_ref[slot, hh] = jnp.exp2(m - m_new)
            m_ref[hh] = m_new
            p_ref[slot, hh] = jnp.exp2((s - m_new).astype(BF16))

    def values(slot, v0):
        for hh in heads:
            pv = _mm(vt_ref[hh, :, pl.ds(v0, tile)], p_ref[slot, hh])
            acc_ref[hh] = alpha_ref[slot, hh] * acc_ref[hh] + pv

    scores(0, 0)
    for hh in heads:
        prev = jnp.broadcast_to(bias_ref[hh, pl.ds(2 * t_own, 1), :], (bs, bs))
        first = jnp.where(own_first, causal_bias, prev)
        second = jnp.where(own_first, NEG_INF, causal_bias)
        s = own_scores[hh] + jnp.concatenate([first, second], axis=0)
        m = jnp.max(s, axis=0, keepdims=True)
        m_ref[hh] = m
        p_ref[1, hh] = jnp.exp2((s - m).astype(BF16))
        alpha_ref[1, hh] = jnp.zeros((1, bs), F32)
        acc_ref[hh] = jnp.zeros((MOBA_VROWS, bs), F32)
    n_pair = (t_own + 1) // 2
    last_tile = nblk // 2 - 1

    def body(u, carry):
        pending = jnp.where(u == 0, r_own, (2 * u - 1) * tile)
        values(1, pl.multiple_of(pending, tile))
        scores(2 * u + 1, 1)
        softmax(0)
        values(0, pl.multiple_of(2 * u * tile, tile))
        scores(jnp.minimum(2 * u + 2, last_tile), 0)
        softmax(1)
        return carry

    lax.fori_loop(0, n_pair, body, 0)
    pending = jnp.where(n_pair == 0, r_own, (2 * n_pair - 1) * tile)
    values(1, pl.multiple_of(pending, tile))
    dh = MOBA_HEAD_DIM
    out = jnp.concatenate([acc_ref[hh, :dh, :] / acc_ref[hh, dh:dh + 1, :] for hh in heads], axis=0)
    o_ref[...] = out.T


def _moba_prompt(qh, kh, vt):
    nseq, _, seqlen, dh = qh.shape
    bs = MOBA_BLOCK
    assert seqlen % (2 * bs) == 0
    nblk = seqlen // bs
    nblk_pad = -(-nblk // 8) * 8
    nh = MOBA_STEP_HEADS
    kern = functools.partial(_moba_prompt_kernel, nblk=nblk, nblk_pad=nblk_pad)
    return pl.pallas_call(
        kern,
        grid=(nseq, MOBA_HEADS // nh, nblk),
        in_specs=[pl.BlockSpec((None, nh, bs, dh), lambda b, h, i: (b, h, i, 0)),
                  pl.BlockSpec((None, nh, seqlen, dh), lambda b, h, i: (b, h, 0, 0)),
                  pl.BlockSpec((None, nh, MOBA_VROWS, seqlen), lambda b, h, i: (b, h, 0, 0))],
        out_specs=pl.BlockSpec((bs, nh * dh), lambda b, h, i: (b * nblk + i, h)),
        out_shape=jax.ShapeDtypeStruct((nseq * seqlen, MOBA_WIDTH), F32),
        scratch_shapes=[pltpu.VMEM((nh, nblk_pad, dh), F32), pltpu.VMEM((nh, nblk_pad, bs), F32),
                        pltpu.VMEM((nh, 1, bs), F32),
                        pltpu.VMEM((nh, MOBA_VROWS, bs), F32), pltpu.VMEM((2, nh, 1, bs), F32),
                        pltpu.VMEM((2, nh, 1, bs), F32),
                        pltpu.VMEM((2, nh, 2 * bs, bs), F32), pltpu.VMEM((2, nh, 2 * bs, bs), BF16)],
        compiler_params=_cparams(("parallel", "parallel", "arbitrary")),
        name="moba_prompt",
    )(qh, kh, vt)


def _moba_gate_kernel(pt_ref, qbd_ref, *rest, ppg):
    del pt_ref
    k_refs = rest[:ppg]
    sel_ref, g_ref = rest[ppg:]
    s_id = pl.program_id(1)
    lane_blk = lax.broadcasted_iota(jnp.int32, (MOBA_HEADS, LANE), 1)

    @pl.when(s_id == 0)
    def _():
        g_ref[...] = jnp.full_like(g_ref, NEG_INF)

    qbd = qbd_ref[...].astype(BF16)
    g = g_ref[...]
    for bi in range(ppg // 2):
        s = jnp.zeros((MOBA_HEADS, PAGE_SIZE), F32)
        for pg in range(2):
            kt = k_refs[2 * bi + pg][...].reshape(MOBA_WIDTH, PAGE_SIZE)
            s = s + _mm(qbd, kt.astype(BF16))
        gate = jnp.sum(s, axis=1, keepdims=True)
        g = jnp.where(lane_blk == s_id * (ppg // 2) + bi, gate, g)
    g_ref[...] = g

    @pl.when(s_id == pl.num_programs(1) - 1)
    def _():
        gg = g_ref[...]
        lane_f = lane_blk.astype(F32)
        picked = jnp.zeros((MOBA_HEADS, LANE), F32)
        for r in range(MOBA_TOPK):
            mx = jnp.max(gg, axis=1, keepdims=True)
            idx = jnp.min(jnp.where(gg == mx, lane_f, float(LANE)), axis=1, keepdims=True)
            picked = jnp.where(lane_blk == r, idx, picked)
            gg = jnp.where(lane_f == idx, NEG_INF, gg)
        sel_ref[...] = picked.astype(jnp.int32)


PICK_HEADS = 8


def _moba_pick_kernel(pt_ref, sel_ref, q_ref, kn_ref, vn_ref, *rest):
    del pt_ref, sel_ref
    npg = 2 * MOBA_TOPK
    k_refs = rest[:PICK_HEADS * npg]
    v_refs = rest[PICK_HEADS * npg:2 * PICK_HEADS * npg]
    o_ref = rest[2 * PICK_HEADS * npg]
    scale = MOBA_HEAD_DIM ** -0.5
    sub = 8

    for hs in range(PICK_HEADS):
        q = q_ref[hs]
        q8 = jnp.broadcast_to((q * scale).astype(BF16), (sub, MOBA_HEAD_DIM))
        s_own = jnp.sum(q * kn_ref[hs], axis=1, keepdims=True) * scale
        s = [_mm(q8, k_refs[hs * npg + j][...].astype(BF16))[0:1, :] for j in range(npg)]
        m = s_own
        for sj in s:
            m = jnp.maximum(m, jnp.max(sj, axis=1, keepdims=True))
        p_own = jnp.exp(s_own - m)
        den = p_own
        num = p_own * vn_ref[hs]
        for j in range(npg):
            p = jnp.exp(s[j] - m)
            den = den + jnp.sum(p, axis=1, keepdims=True)
            p8 = jnp.broadcast_to(p.astype(BF16), (sub, PAGE_SIZE))
            num = num + _nt(p8, v_refs[hs * npg + j][...].astype(BF16))[0:1, :]
        o_ref[hs] = num / den


def _moba_decode(page_table, q, k_new, v_new, cache_kt, cache_vt, layer):
    nseq, n_pages = page_table.shape
    nblk = n_pages * PAGE_SIZE // MOBA_BLOCK
    assert n_pages * PAGE_SIZE == nblk * MOBA_BLOCK and MOBA_TOPK <= nblk <= LANE
    ppg = 16 if n_pages % 16 == 0 else 2
    heads, dh = MOBA_HEADS, MOBA_HEAD_DIM

    qbd = (q[:, :, None, :] * jnp.eye(heads, dtype=F32)[None, :, :, None]).reshape(nseq, heads, heads * dh)

    def all_heads(p):
        return pl.BlockSpec((None, None, heads, dh, PAGE_SIZE),
                            lambda b, s, pt, p=p: (pt[b, s * ppg + p], layer, 0, 0, 0))

    sel = pl.pallas_call(
        functools.partial(_moba_gate_kernel, ppg=ppg),
        grid_spec=pltpu.PrefetchScalarGridSpec(
            num_scalar_prefetch=1,
            grid=(nseq, n_pages // ppg),
            in_specs=[pl.BlockSpec((None, heads, heads * dh), lambda b, s, pt: (b, 0, 0))]
            + [all_heads(p) for p in range(ppg)],
            out_specs=pl.BlockSpec((None, heads, LANE), lambda b, s, pt: (b, 0, 0)),
            scratch_shapes=[pltpu.VMEM((heads, LANE), F32)],
        ),
        out_shape=jax.ShapeDtypeStruct((nseq, heads, LANE), jnp.int32),
        compiler_params=_cparams(("parallel", "arbitrary")),
        name="moba_gate",
    )(page_table, qbd, *([cache_kt] * ppg))
    sel_flat = sel[:, :, :MOBA_TOPK].reshape(nseq * heads * MOBA_TOPK)

    def one_head(hs, j):
        def index(b, g, pt, sl):
            h = g * PICK_HEADS + hs
            blk = sl[(b * heads + h) * MOBA_TOPK + j // 2]
            return (pt[b, 2 * blk + j % 2], layer, h, 0, 0)
        return pl.BlockSpec((None, None, None, dh, PAGE_SIZE), index)

    row = pl.BlockSpec((None, PICK_HEADS, 1, dh), lambda b, g, pt, sl: (b, g, 0, 0))
    rows = lambda a: a.reshape(nseq, heads, 1, dh)
    pages = [one_head(hs, j) for hs in range(PICK_HEADS) for j in range(2 * MOBA_TOPK)]
    out = pl.pallas_call(
        _moba_pick_kernel,
        grid_spec=pltpu.PrefetchScalarGridSpec(
            num_scalar_prefetch=2,
            grid=(nseq, heads // PICK_HEADS),
            in_specs=[row, row, row] + pages + pages,
            out_specs=row,
        ),
        out_shape=jax.ShapeDtypeStruct((nseq, heads, 1, dh), F32),
        compiler_params=_cparams(("parallel", "parallel")),
        name="moba_pick",
    )(page_table, sel_flat, rows(q), rows(k_new), rows(v_new),
      *([cache_kt] * len(pages)), *([cache_vt] * len(pages)))
    return out.reshape(nseq, heads * dh)


def _s5_disc_kernel(are_ref, aim_ref, ls_ref, bre_ref, bim_ref, abr_ref, abi_ref, bbr_ref, bbi_ref):
    a_re = are_ref[...]
    a_im = aim_ref[...]
    step = jnp.exp(ls_ref[...])
    mag = jnp.exp(a_re * step)
    ab_re = mag * jnp.cos(a_im * step)
    ab_im = mag * jnp.sin(a_im * step)
    den = a_re * a_re + a_im * a_im
    n_re = ab_re - 1.0
    f_re = (n_re * a_re + ab_im * a_im) / den
    f_im = (ab_im * a_re - n_re * a_im) / den
    b_re = bre_ref[...]
    b_im = bim_ref[...]
    abr_ref[...] = ab_re
    abi_ref[...] = ab_im
    bbr_ref[...] = f_re * b_re - f_im * b_im
    bbi_ref[...] = f_re * b_im + f_im * b_re


def _s5_discretise(a_re, a_im, log_step, b_re, b_im):
    depth = a_re.shape[0]
    rows = depth * S5_GROUPS * S5_GROUP
    rep = lambda a: jnp.broadcast_to(a[:, :, None, :], (depth, S5_GROUPS, S5_GROUP, S5_STATE)).reshape(rows, S5_STATE)
    ls = jnp.broadcast_to(log_step[:, :, None, None], (depth, S5_GROUPS, S5_GROUP, S5_STATE)).reshape(rows, S5_STATE)
    bt = lambda b: jnp.swapaxes(b, 2, 3).reshape(rows, S5_STATE)
    shp = jax.ShapeDtypeStruct((rows, S5_STATE), F32)
    abr, abi, bbr, bbi = pl.pallas_call(
        _s5_disc_kernel, out_shape=[shp] * 4, name="s5_disc",
    )(rep(a_re), rep(a_im), ls, bt(b_re), bt(b_im))
    r4 = lambda a: a.reshape(depth, S5_GROUPS, S5_GROUP, S5_STATE)
    ab = jnp.concatenate([r4(abr)[:, :, 0].reshape(depth, 1, S5_CH),
                          r4(abi)[:, :, 0].reshape(depth, 1, S5_CH)], axis=1)
    eye = jnp.eye(S5_GROUPS, dtype=F32)
    bd = lambda a: (r4(a)[:, :, :, None, :] * eye[None, :, None, :, None]).reshape(depth, S5_WIDTH, S5_CH)
    bb = jnp.concatenate([bd(bbr), bd(bbi)], axis=2)
    return ab, bb.astype(BF16)


def _s5_out_map(c_re, c_im):
    depth = c_re.shape[0]
    eye = jnp.eye(S5_GROUPS, dtype=F32)
    bd = lambda c: (jnp.swapaxes(c, 2, 3)[:, :, :, None, :] * eye[None, :, None, :, None]).reshape(depth, S5_CH, S5_WIDTH)
    return jnp.concatenate([bd(c_re), -bd(c_im)], axis=1).astype(BF16)


S5_SCAN_LANES = 512


def _s5_kernel(u_ref, bb_ref, ab_ref, h0_ref, cc_ref, d_ref, wglu_ref, bglu_ref,
               o_ref, hfin_ref, xs_ref, hc_ref, *, nb, tt):
    @pl.when(pl.program_id(0) == 0)
    def _():
        hc_ref[...] = h0_ref[...]

    u = u_ref[...]
    xs_ref[...] = _mm(u.astype(BF16), bb_ref[...])

    for c0 in range(0, S5_CH, S5_SCAN_LANES):
        re_sl = slice(c0, c0 + S5_SCAN_LANES)
        im_sl = slice(S5_CH + c0, S5_CH + c0 + S5_SCAN_LANES)
        ar = jnp.broadcast_to(ab_ref[0:1, re_sl], (nb, S5_SCAN_LANES))
        ai = jnp.broadcast_to(ab_ref[1:2, re_sl], (nb, S5_SCAN_LANES))

        def step(t, carry, re_sl=re_sl, im_sl=im_sl, ar=ar, ai=ai):
            hr, hi = carry
            rows = pl.ds(pl.multiple_of(t * nb, nb), nb)
            nhr = ar * hr - ai * hi + xs_ref[rows, re_sl]
            nhi = ar * hi + ai * hr + xs_ref[rows, im_sl]
            xs_ref[rows, re_sl] = nhr
            xs_ref[rows, im_sl] = nhi
            return nhr, nhi

        hr, hi = lax.fori_loop(0, tt, step, (hc_ref[:, re_sl], hc_ref[:, im_sl]), unroll=min(4, tt))
        hc_ref[:, re_sl] = hr
        hc_ref[:, im_sl] = hi

    y = _mm(xs_ref[...].astype(BF16), cc_ref[...]) + d_ref[...] * u
    yg = _gelu_tanh(y)
    gate = _sigmoid(_mm(yg.astype(BF16), wglu_ref[...]) + bglu_ref[...])
    o_ref[...] = yg * gate
    hfin_ref[...] = hc_ref[...]


def _s5(u_tm, nb, bb, ab, h0, cc, d, wglu, bglu):
    rows = u_tm.shape[0]
    t_total = rows // nb
    tt = min(128, t_total)
    const = lambda shape: pl.BlockSpec(shape, lambda t: (0,) * len(shape))
    kern = functools.partial(_s5_kernel, nb=nb, tt=tt)
    return pl.pallas_call(
        kern,
        grid=(t_total // tt,),
        in_specs=[pl.BlockSpec((tt * nb, S5_WIDTH), lambda t: (t, 0)),
                  const((S5_WIDTH, 2 * S5_CH)), const((2, S5_CH)), const((nb, 2 * S5_CH)),
                  const((2 * S5_CH, S5_WIDTH)), const((1, S5_WIDTH)),
                  const((S5_WIDTH, S5_WIDTH)), const((1, S5_WIDTH))],
        out_specs=[pl.BlockSpec((tt * nb, S5_WIDTH), lambda t: (t, 0)), const((nb, 2 * S5_CH))],
        out_shape=[jax.ShapeDtypeStruct((rows, S5_WIDTH), F32), jax.ShapeDtypeStruct((nb, 2 * S5_CH), F32)],
        scratch_shapes=[pltpu.VMEM((tt * nb, 2 * S5_CH), F32), pltpu.VMEM((nb, 2 * S5_CH), F32)],
        compiler_params=_cparams(("arbitrary",)),
        name="s5_scan",
    )(u_tm, bb, ab, h0, cc, d, wglu, bglu)


def _split3(x):
    hi = x.astype(BF16)
    r = x - hi.astype(F32)
    mid = r.astype(BF16)
    lo = (r - mid.astype(F32)).astype(BF16)
    return hi, mid, lo


def _gla_prompt_kernel(gq_ref, gk_ref, gv_ref, gr_ref, lf_ref, gn_ref, o_ref, sfin_ref, st_ref, *, nchunk):
    c = GLA_CHUNK

    @pl.when(pl.program_id(1) == 0)
    def _():
        st_ref[...] = jnp.zeros_like(st_ref)

    row = lax.broadcasted_iota(jnp.int32, (c, c), 0)
    col = lax.broadcasted_iota(jnp.int32, (c, c), 1)
    causal = col <= row
    tri = causal.astype(BF16)
    gn = gn_ref[...]

    for ci in range(nchunk):
        rows = slice(ci * c, (ci + 1) * c)
        g_hi, g_mid, g_lo = _split3(lf_ref[rows, :])
        bcum = _mm(tri, g_hi) + _mm(tri, g_mid) + _mm(tri, g_lo)
        blast = bcum[c - 1:c, :]
        qd = (gq_ref[rows, :] * jnp.exp(bcum)).astype(BF16)
        kd = (gk_ref[rows, :] * jnp.exp(-bcum)).astype(BF16)
        kt = (gk_ref[rows, :] * jnp.exp(blast - bcum)).astype(BF16)
        st = st_ref[...]
        st_ref[...] = st * jnp.exp(blast)
        st16 = st.astype(BF16)
        for hd in range(GLA_HEADS):
            ks = slice(hd * GLA_DK, (hd + 1) * GLA_DK)
            vs = slice(hd * GLA_DV, (hd + 1) * GLA_DV)
            vh = gv_ref[rows, vs].astype(BF16)
            att = jnp.where(causal, _nt(qd[:, ks], kd[:, ks]), 0.0)
            o = _nt(qd[:, ks], st16[:, ks]) + _mm(att.astype(BF16), vh)
            st_ref[:, ks] += _tn(vh, kt[:, ks])
            o_ref[rows, vs] = _rms(o, gn) * _silu(gr_ref[rows, vs])

    sfin_ref[...] = st_ref[...]


def _gla_prompt(gq, gk, gv, gr, lf, gnorm, nseq, seqlen):
    assert seqlen % GLA_CHUNK == 0
    tt = min(512, seqlen)
    nt = seqlen // tt
    tok = lambda width: pl.BlockSpec((tt, width), lambda b, t: (b * nt + t, 0))
    kern = functools.partial(_gla_prompt_kernel, nchunk=tt // GLA_CHUNK)
    return pl.pallas_call(
        kern,
        grid=(nseq, nt),
        in_specs=[tok(GLA_KEY_WIDTH), tok(GLA_KEY_WIDTH), tok(GLA_VAL_WIDTH), tok(GLA_VAL_WIDTH),
                  tok(GLA_KEY_WIDTH), pl.BlockSpec((1, GLA_DV), lambda b, t: (0, 0))],
        out_specs=[tok(GLA_VAL_WIDTH),
                   pl.BlockSpec((None, GLA_DV, GLA_KEY_WIDTH), lambda b, t: (b, 0, 0))],
        out_shape=[jax.ShapeDtypeStruct((nseq * seqlen, GLA_VAL_WIDTH), F32),
                   jax.ShapeDtypeStruct((nseq, GLA_DV, GLA_KEY_WIDTH), F32)],
        scratch_shapes=[pltpu.VMEM((GLA_DV, GLA_KEY_WIDTH), F32)],
        compiler_params=_cparams(("parallel", "arbitrary")),
        name="gla_prompt",
    )(gq, gk, gv, gr, lf, gnorm)


def _gla_decode_kernel(q_ref, k_ref, g_ref, v_ref, r_ref, s0_ref, gn_ref, o_ref, s_ref):
    s = jnp.exp(g_ref[...]) * s0_ref[...] + k_ref[...] * v_ref[...]
    s_ref[...] = s
    o = jnp.sum(q_ref[...] * s, axis=2, keepdims=True)
    o_ref[...] = _rms(o, gn_ref[...]) * _silu(r_ref[...])


def _gla_decode(gq, gk, lf, gv, gr, s0, gnorm):
    n = gq.shape[0]
    col = lambda a: a.reshape(n, GLA_HEADS, GLA_DK, 1)
    rowv = lambda a: a.reshape(n, GLA_HEADS, 1, GLA_DV)
    g = 8 if n % 8 == 0 else 1
    cspec = pl.BlockSpec((g, GLA_HEADS, GLA_DK, 1), lambda b: (b, 0, 0, 0))
    rspec = pl.BlockSpec((g, GLA_HEADS, 1, GLA_DV), lambda b: (b, 0, 0, 0))
    sspec = pl.BlockSpec((g, GLA_HEADS, GLA_DK, GLA_DV), lambda b: (b, 0, 0, 0))
    o, s = pl.pallas_call(
        _gla_decode_kernel,
        grid=(n // g,),
        in_specs=[cspec, cspec, cspec, rspec, rspec, sspec,
                  pl.BlockSpec((1, 1, 1, GLA_DV), lambda b: (0, 0, 0, 0))],
        out_specs=[rspec, sspec],
        out_shape=[jax.ShapeDtypeStruct((n, GLA_HEADS, 1, GLA_DV), F32),
                   jax.ShapeDtypeStruct((n, GLA_HEADS, GLA_DK, GLA_DV), F32)],
        compiler_params=_cparams(("parallel",)),
        name="gla_decode",
    )(col(gq), col(gk), col(lf), rowv(gv), rowv(gr), s0, gnorm.reshape(1, 1, 1, GLA_DV))
    return o.reshape(n, GLA_VAL_WIDTH), s


def _out_ffn_kernel(x_ref, a_ref, b_ref, c_ref, wo_ref, nf_ref, wg_ref, wu_ref, wd_ref, nfin_ref,
                    o_ref, hf_ref, acc_ref, *, final_norm):
    j = pl.program_id(2)

    @pl.when(j == 0)
    def _():
        mix = jnp.concatenate([a_ref[...], b_ref[...], c_ref[...]], axis=1).astype(BF16)
        x1 = x_ref[...] + _mm(mix, wo_ref[...])
        acc_ref[...] = x1
        hf_ref[...] = _rms(x1, nf_ref[...]).astype(BF16)

    hf = hf_ref[...]
    act = _silu(_mm(hf, wg_ref[...])) * _mm(hf, wu_ref[...])
    acc_ref[...] += _mm(act.astype(BF16), wd_ref[...])

    @pl.when(j == pl.num_programs(2) - 1)
    def _():
        y = acc_ref[...]
        o_ref[...] = _rms(y, nfin_ref[...]) if final_norm else y


def _out_ffn(x2d, a, b_tm, c, nseq, seqlen, w_out, norm_ffn, w_gate, w_up, w_down, norm_final, final_norm):
    tm = min(512, seqlen)
    nt = seqlen // tm
    th = FFN_HIDDEN // 2
    nh = FFN_HIDDEN // th
    tok = lambda width: pl.BlockSpec((tm, width), lambda b, t, j: (b * nt + t, 0))
    const = lambda shape: pl.BlockSpec(shape, lambda b, t, j: (0,) * len(shape))
    kern = functools.partial(_out_ffn_kernel, final_norm=final_norm)
    return pl.pallas_call(
        kern,
        grid=(nseq, nt, nh),
        in_specs=[tok(D_MODEL), tok(MOBA_WIDTH),
                  pl.BlockSpec((tm, S5_WIDTH), lambda b, t, j: (t, b)),
                  tok(GLA_VAL_WIDTH), const((D_MODEL, D_MODEL)), const((1, D_MODEL)),
                  pl.BlockSpec((D_MODEL, th), lambda b, t, j: (0, j)),
                  pl.BlockSpec((D_MODEL, th), lambda b, t, j: (0, j)),
                  pl.BlockSpec((th, D_MODEL), lambda b, t, j: (j, 0)),
                  const((1, D_MODEL))],
        out_specs=tok(D_MODEL),
        out_shape=jax.ShapeDtypeStruct((nseq * seqlen, D_MODEL), F32),
        scratch_shapes=[pltpu.VMEM((tm, D_MODEL), BF16), pltpu.VMEM((tm, D_MODEL), F32)],
        compiler_params=_cparams(("parallel", "parallel", "arbitrary")),
        name="out_ffn",
    )(x2d, a, b_tm, c, w_out, norm_ffn, w_gate, w_up, w_down, norm_final)


def _prep_weights(w_in, w_out, s5_w_glu, gla_w_gate, w_ffn_gate, w_ffn_up, w_ffn_down):
    depth = w_in.shape[0]
    w_in_p = jnp.pad(w_in, ((0, 0), (0, 0), (0, IN_COLS_PAD - IN_COLS))).astype(BF16)
    wg_p = jnp.pad(gla_w_gate, ((0, 0), (0, LANE - GLA_GATE_RANK), (0, 0))).astype(BF16)
    del depth
    return (w_in_p, w_out.astype(BF16), s5_w_glu.astype(BF16), wg_p,
            w_ffn_gate.astype(BF16), w_ffn_up.astype(BF16), w_ffn_down.astype(BF16))


def _trunk(x, kv_pool, ssm_re0, ssm_im0, gla0, params):
    (norm_mix, w_in_p, w_out, ab, bb, cc, s5_d, w_glu, s5_b_glu, wg_p, gla_b_gate, gla_norm,
     norm_ffn, w_gate, w_up, w_down, norm_final) = params
    nseq, seqlen, _ = x.shape
    depth = w_in_p.shape[0]
    decode = kv_pool is not None
    if decode:
        assert seqlen == 1
        cache_k, cache_v, page_table = kv_pool
        cache_kt = jnp.transpose(cache_k, (0, 1, 3, 4, 2))
        cache_vt = jnp.transpose(cache_v, (0, 1, 3, 4, 2))
        tw_nseq, tw_len = 1, nseq
    else:
        assert seqlen % MOBA_BLOCK == 0
        tw_nseq, tw_len = nseq, seqlen
    n = nseq * seqlen
    x2d = x.reshape(n, D_MODEL)
    k_rows, v_rows, s_re, s_im, s_gla = [], [], [], [], []
    for l in range(depth):
        k, v, *q_parts, u_tm, gq, gk, gv, gr, lf = _in_proj(
            x2d, tw_nseq, tw_len, norm_mix[l][None], w_in_p[l], wg_p[l], gla_b_gate[l][None],
            head_major=not decode, layer=l, depth=depth, kv_stack=(k_rows[-1], v_rows[-1]) if l else None)
        k_rows.append(k)
        v_rows.append(v)
        if decode:
            heads = lambda a: a.reshape(nseq, MOBA_HEADS, MOBA_HEAD_DIM)
            a_out = _moba_decode(page_table, heads(q_parts[0]), heads(k), heads(v), cache_kt, cache_vt, l)
        else:
            a_out = _moba_prompt(*q_parts)
        if ssm_re0 is None:
            h0 = jnp.zeros((nseq, 2 * S5_CH), F32)
        else:
            h0 = jnp.concatenate([ssm_re0[l].reshape(nseq, S5_CH), ssm_im0[l].reshape(nseq, S5_CH)], axis=1)
        b_tm, h_fin = _s5(u_tm.reshape(n, S5_WIDTH), nseq, bb[l], ab[l], h0, cc[l], s5_d[l][None],
                          w_glu[l], s5_b_glu[l][None])
        s_re.append(h_fin[:, :S5_CH].reshape(nseq, S5_GROUPS, S5_STATE))
        s_im.append(h_fin[:, S5_CH:].reshape(nseq, S5_GROUPS, S5_STATE))
        b_tm = b_tm.reshape(tw_len, tw_nseq * S5_WIDTH)
        if decode:
            c_out, s_fin = _gla_decode(gq, gk, lf, gv, gr, gla0[l], gla_norm[l])
        else:
            c_out, s_t = _gla_prompt(gq, gk, gv, gr, lf, gla_norm[l][None], nseq, seqlen)
            s_fin = jnp.transpose(s_t.reshape(nseq, GLA_DV, GLA_HEADS, GLA_DK), (0, 2, 3, 1))
        s_gla.append(s_fin)
        x2d = _out_ffn(x2d, a_out, b_tm, c_out, tw_nseq, tw_len, w_out[l], norm_ffn[l][None],
                       w_gate[l], w_up[l], w_down[l], norm_final[None], final_norm=(l == depth - 1))
    if decode:
        shp = (nseq, seqlen, MOBA_HEADS, MOBA_HEAD_DIM)
        stack_kv = lambda rows: jnp.stack([r.reshape(shp) for r in rows], axis=1)
    else:
        shp = (nseq, depth, MOBA_HEADS, MOBA_HEAD_DIM, seqlen)
        stack_kv = lambda rows: jnp.transpose(rows[-1].reshape(shp), (0, 1, 4, 2, 3))
    return (x2d.reshape(nseq, seqlen, D_MODEL), stack_kv(k_rows), stack_kv(v_rows),
            jnp.stack(s_re, axis=0), jnp.stack(s_im, axis=0), jnp.stack(s_gla, axis=0))


def kernel(x_prompt, x_sample, cache_k, cache_v, page_table, state_ssm_re, state_ssm_im, state_gla,
           norm_mix, w_in, w_out, s5_a_re, s5_a_im, s5_log_step, s5_b_re, s5_b_im, s5_c_re, s5_c_im,
           s5_d, s5_w_glu, s5_b_glu, gla_w_gate, gla_b_gate, gla_norm, norm_ffn, w_ffn_gate, w_ffn_up,
           w_ffn_down, norm_final):
    w_in_p, w_out16, w_glu16, wg_p, w_gate16, w_up16, w_down16 = _prep_weights(
        w_in, w_out, s5_w_glu, gla_w_gate, w_ffn_gate, w_ffn_up, w_ffn_down)
    ab, bb = _s5_discretise(s5_a_re, s5_a_im, s5_log_step, s5_b_re, s5_b_im)
    cc = _s5_out_map(s5_c_re, s5_c_im)
    params = (norm_mix, w_in_p, w_out16, ab, bb, cc, s5_d, w_glu16, s5_b_glu, wg_p, gla_b_gate, gla_norm,
              norm_ffn, w_gate16, w_up16, w_down16, norm_final)
    y_p, k_p, v_p, re_p, im_p, gla_p = _trunk(x_prompt, None, None, None, None, params)
    y_s, k_s, v_s, re_s, im_s, gla_s = _trunk(x_sample, (cache_k, cache_v, page_table),
                                              state_ssm_re, state_ssm_im, state_gla, params)
    return (y_p, y_s, k_p, v_p, k_s, v_s, re_p, im_p, re_s, im_s, gla_p, gla_s)
```

```python
import functools
import math

import jax
import jax.numpy as jnp
from jax import lax
from jax.experimental import pallas as pl
from jax.experimental.pallas import tpu as pltpu

F32 = jnp.float32
BF16 = jnp.bfloat16

D_MODEL = 1024
PAGE_SIZE = 128
MOBA_HEAD_DIM = 64
MOBA_HEADS = 8
MOBA_WIDTH = MOBA_HEADS * MOBA_HEAD_DIM
MOBA_BLOCK = 256
MOBA_TOPK = 3
MOBA_VROWS = 80
LOG2E = 1.4426950408889634
S5_WIDTH = 256
S5_GROUP = 16
S5_GROUPS = 16
S5_STATE = 64
S5_CH = S5_GROUPS * S5_STATE
GLA_HEADS = 4
GLA_DK = 32
GLA_DV = 64
GLA_KEY_WIDTH = GLA_HEADS * GLA_DK
GLA_VAL_WIDTH = GLA_HEADS * GLA_DV
GLA_GATE_RANK = 16
GLA_GATE_NORM = 16.0
GLA_CHUNK = 64
FFN_HIDDEN = 2816
RMS_EPS = 1e-6
NEG_INF = float("-inf")

_OFF_Q = 0
_OFF_K = _OFF_Q + MOBA_WIDTH
_OFF_V = _OFF_K + MOBA_WIDTH
_OFF_U = _OFF_V + MOBA_WIDTH
_OFF_GQ = _OFF_U + S5_WIDTH
_OFF_GK = _OFF_GQ + GLA_KEY_WIDTH
_OFF_GV = _OFF_GK + GLA_KEY_WIDTH
_OFF_GR = _OFF_GV + GLA_VAL_WIDTH
_OFF_GLR = _OFF_GR + GLA_VAL_WIDTH
IN_COLS = _OFF_GLR + GLA_GATE_RANK
LANE = 128
IN_COLS_PAD = _OFF_GLR + LANE

VMEM_LIMIT = 56 * 1024 * 1024


def _cparams(sem):
    return pltpu.CompilerParams(dimension_semantics=sem, vmem_limit_bytes=VMEM_LIMIT)


def _nt(a, b):
    return lax.dot_general(a, b, (((1,), (1,)), ((), ())), preferred_element_type=F32)


def _tn(a, b):
    return lax.dot_general(a, b, (((0,), (0,)), ((), ())), preferred_element_type=F32)


def _mm(a, b):
    return jnp.dot(a, b, preferred_element_type=F32)


def _sigmoid(x):
    return 1.0 / (1.0 + jnp.exp(-x))


def _silu(x):
    return x * _sigmoid(x)


def _gelu_tanh(x):
    c = math.sqrt(2.0 / math.pi)
    return 0.5 * x * (1.0 + jnp.tanh(c * (x + 0.044715 * (x * x * x))))


def _log_sigmoid(x):
    return jnp.minimum(x, 0.0) - jnp.log(1.0 + jnp.exp(-jnp.abs(x)))


def _rms(x, g):
    return x * lax.rsqrt(jnp.mean(x * x, axis=-1, keepdims=True) + RMS_EPS) * g


def _in_proj_kernel(x_ref, g_ref, w_ref, wg_ref, bg_ref, *rest, head_major, layer):
    if head_major and layer > 0:
        rest = rest[2:]
    k_ref, v_ref = rest[:2]
    rest = rest[2:]
    h = _rms(x_ref[...], g_ref[...]).astype(BF16)

    def proj(lo, width):
        return _mm(h, w_ref[:, lo:lo + width])

    q = proj(_OFF_Q, MOBA_WIDTH)
    k = proj(_OFF_K, MOBA_WIDTH)
    v = proj(_OFF_V, MOBA_WIDTH)
    if head_major:
        vt = v.T
        if layer == 0:
            k_ref[0] = k.T
            v_ref[0] = vt
            for later in range(1, k_ref.shape[0]):
                k_ref[later] = jnp.zeros(vt.shape, F32)
                v_ref[later] = jnp.zeros(vt.shape, F32)
        else:
            k_ref[...] = k.T
            v_ref[...] = vt
        qh_ref, kh_ref, vt_ref = rest[:3]
        rest = rest[3:]
        q16 = (q * (MOBA_HEAD_DIM ** -0.5 * LOG2E)).astype(BF16)
        k16 = k.astype(BF16)
        vt16 = vt.astype(BF16)
        for hd in range(MOBA_HEADS):
            sl = slice(hd * MOBA_HEAD_DIM, (hd + 1) * MOBA_HEAD_DIM)
            qh_ref[hd] = q16[:, sl]
            kh_ref[hd] = k16[:, sl]
            vt_ref[hd, :MOBA_HEAD_DIM, :] = vt16[sl, :]
            vt_ref[hd, MOBA_HEAD_DIM:, :] = jnp.ones((MOBA_VROWS - MOBA_HEAD_DIM, vt16.shape[1]), BF16)
    else:
        k_ref[...] = k
        v_ref[...] = v
        rest[0][...] = q
        rest = rest[1:]
    u_ref, gq_ref, gk_ref, gv_ref, gr_ref, lf_ref = rest
    u_ref[...] = proj(_OFF_U, S5_WIDTH)
    gq_ref[...] = proj(_OFF_GQ, GLA_KEY_WIDTH) * (GLA_DK ** -0.5)
    gk_ref[...] = proj(_OFF_GK, GLA_KEY_WIDTH)
    gv_ref[...] = proj(_OFF_GV, GLA_VAL_WIDTH)
    gr_ref[...] = proj(_OFF_GR, GLA_VAL_WIDTH)
    glr = proj(_OFF_GLR, LANE)
    gate = _mm(glr.astype(BF16), wg_ref[...]) + bg_ref[...]
    lf_ref[...] = _log_sigmoid(gate) * (1.0 / GLA_GATE_NORM)


def _in_proj(x2d, nseq, seqlen, norm_g, w_pad, wg_pad, bg, head_major, layer=0, depth=1, kv_stack=None):
    n = nseq * seqlen
    tm = min(512, seqlen)
    nt = seqlen // tm
    tok = lambda width: pl.BlockSpec((tm, width), lambda b, t: (b * nt + t, 0))
    const = lambda shape: pl.BlockSpec(shape, lambda b, t: (0,) * len(shape))
    tokf = lambda width: jax.ShapeDtypeStruct((n, width), F32)
    extra_in, extra_specs, aliases = [], [], {}
    if head_major:
        head = pl.BlockSpec((None, MOBA_HEADS, tm, MOBA_HEAD_DIM), lambda b, t: (b, 0, t, 0))
        head_t = pl.BlockSpec((None, MOBA_HEADS, MOBA_VROWS, tm), lambda b, t: (b, 0, 0, t))
        if layer == 0:
            kv_t = pl.BlockSpec((None, depth, MOBA_WIDTH, tm), lambda b, t: (b, 0, 0, t))
        else:
            kv_t = pl.BlockSpec((None, None, MOBA_WIDTH, tm), lambda b, t: (b, layer, 0, t))
            extra_in = list(kv_stack)
            extra_specs = [pl.BlockSpec(memory_space=pl.ANY)] * 2
            aliases = {5: 0, 6: 1}
        q_specs = [kv_t, kv_t, head, head, head_t]
        q_shapes = [jax.ShapeDtypeStruct((nseq, depth, MOBA_WIDTH, seqlen), F32)] * 2 + [
            jax.ShapeDtypeStruct((nseq, MOBA_HEADS, seqlen, MOBA_HEAD_DIM), BF16)] * 2 + [
            jax.ShapeDtypeStruct((nseq, MOBA_HEADS, MOBA_VROWS, seqlen), BF16)]
    else:
        q_specs = [tok(MOBA_WIDTH), tok(MOBA_WIDTH), tok(MOBA_WIDTH)]
        q_shapes = [tokf(MOBA_WIDTH)] * 3
    return pl.pallas_call(
        functools.partial(_in_proj_kernel, head_major=head_major, layer=layer),
        grid=(nseq, nt),
        in_specs=[tok(D_MODEL), const((1, D_MODEL)), const((D_MODEL, IN_COLS_PAD)),
                  const((LANE, GLA_KEY_WIDTH)), const((1, GLA_KEY_WIDTH))] + extra_specs,
        input_output_aliases=aliases,
        out_specs=q_specs + [
            pl.BlockSpec((tm, S5_WIDTH), lambda b, t: (t, b)),
            tok(GLA_KEY_WIDTH), tok(GLA_KEY_WIDTH), tok(GLA_VAL_WIDTH), tok(GLA_VAL_WIDTH),
            tok(GLA_KEY_WIDTH)],
        out_shape=q_shapes + [
            jax.ShapeDtypeStruct((seqlen, nseq * S5_WIDTH), F32),
            tokf(GLA_KEY_WIDTH), tokf(GLA_KEY_WIDTH), tokf(GLA_VAL_WIDTH), tokf(GLA_VAL_WIDTH),
            tokf(GLA_KEY_WIDTH)],
        compiler_params=_cparams(("parallel", "parallel")),
        name="in_proj",
    )(x2d, norm_g, w_pad, wg_pad, bg, *extra_in)


MOBA_STEP_HEADS = 8


def _moba_prompt_kernel(q_ref, k_ref, vt_ref, o_ref, means_ref, bias_ref, m_ref, acc_ref, alpha_ref, tmax_ref,
                        s_ref, p_ref, *, nblk, nblk_pad):
    i = pl.program_id(2)
    bs = MOBA_BLOCK
    heads = range(MOBA_STEP_HEADS)
    tile = 2 * bs

    @pl.when(i == 0)
    def _():
        means_ref[...] = jnp.zeros_like(means_ref)
        for hh in heads:
            for j in range(nblk):
                kj = k_ref[hh, j * bs:(j + 1) * bs, :].astype(F32)
                means_ref[hh, j:j + 1, :] = jnp.mean(kj, axis=0, keepdims=True)

    blk_iota = lax.broadcasted_iota(jnp.int32, (nblk_pad, bs), 0)
    blk_f = blk_iota.astype(F32)
    key_iota = lax.broadcasted_iota(jnp.int32, (bs, bs), 0)
    qry_iota = lax.broadcasted_iota(jnp.int32, (bs, bs), 1)
    causal_bias = jnp.where(key_iota <= qry_iota, 0.0, NEG_INF)
    own_first = (i % 2) == 0
    t_own = i // 2
    r_own = pl.multiple_of(t_own * tile, tile)

    gates = [_nt(means_ref[hh].astype(BF16), q_ref[hh]) for hh in heads]
    own_scores = [_nt(k_ref[hh, pl.ds(r_own, tile), :], q_ref[hh]) for hh in heads]
    past = blk_iota < i
    for hh in heads:
        gate = jnp.where(past, gates[hh], NEG_INF)
        chosen = jnp.zeros(gate.shape, jnp.bool_)
        for _ in range(MOBA_TOPK):
            mx = jnp.max(gate, axis=0, keepdims=True)
            idx = jnp.min(jnp.where(gate == mx, blk_f, float(nblk_pad)), axis=0, keepdims=True)
            pick = blk_f == idx
            chosen = chosen | pick
            gate = jnp.where(pick, NEG_INF, gate)
        bias_ref[hh] = jnp.where(chosen & past, 0.0, NEG_INF)

    def scores(t, slot):
        c0 = t * tile if isinstance(t, int) else pl.multiple_of(t * tile, tile)
        live = t < t_own
        for hh in heads:
            s = _nt(k_ref[hh, pl.ds(c0, tile), :], q_ref[hh])
            r0 = jnp.where(live, bias_ref[hh, pl.ds(2 * t, 1), :], NEG_INF)
            r1 = jnp.where(live, bias_ref[hh, pl.ds(2 * t + 1, 1), :], NEG_INF)
            s0 = s[:bs] + r0
            s1 = s[bs:] + r1
            s_ref[slot, hh, :bs, :] = s0
            s_ref[slot, hh, bs:, :] = s1
            tmax_ref[slot, hh] = jnp.max(jnp.maximum(s0, s1), axis=0, keepdims=True)

    def softmax(slot):
        for hh in heads:
            s = s_ref[slot, hh]
            m = m_ref[hh]
            m_new = jnp.maximum(m, tmax_ref[slot, hh])
            alpha_ref[slot, hh] = jnp.exp2(m - m_new)
            m_ref[hh] = m_new
            p_ref[slot, hh] = jnp.exp2((s - m_new).astype(BF16))

    def values(slot, v0):
        for hh in heads:
            pv = _mm(vt_ref[hh, :, pl.ds(v0, tile)], p_ref[slot, hh])
            acc_ref[hh] = alpha_ref[slot, hh] * acc_ref[hh] + pv

    scores(0, 0)
    for hh in heads:
        prev = jnp.broadcast_to(bias_ref[hh, pl.ds(2 * t_own, 1), :], (bs, bs))
        first = jnp.where(own_first, causal_bias, prev)
        second = jnp.where(own_first, NEG_INF, causal_bias)
        s = own_scores[hh] + jnp.concatenate([first, second], axis=0)
        m = jnp.max(s, axis=0, keepdims=True)
        m_ref[hh] = m
        p_ref[1, hh] = jnp.exp2((s - m).astype(BF16))
        alpha_ref[1, hh] = jnp.zeros((1, bs), F32)
        acc_ref[hh] = jnp.zeros((MOBA_VROWS, bs), F32)
    n_pair = (t_own + 1) // 2
    last_tile = nblk // 2 - 1

    def body(u, carry):
        pending = jnp.where(u == 0, r_own, (2 * u - 1) * tile)
        values(1, pl.multiple_of(pending, tile))
        scores(2 * u + 1, 1)
        softmax(0)
        values(0, pl.multiple_of(2 * u * tile, tile))
        scores(jnp.minimum(2 * u + 2, last_tile), 0)
        softmax(1)
        return carry

    lax.fori_loop(0, n_pair, body, 0)
    pending = jnp.where(n_pair == 0, r_own, (2 * n_pair - 1) * tile)
    values(1, pl.multiple_of(pending, tile))
    dh = MOBA_HEAD_DIM
    out = jnp.concatenate([acc_ref[hh, :dh, :] / acc_ref[hh, dh:dh + 1, :] for hh in heads], axis=0)
    o_ref[...] = out.T


def _moba_prompt(qh, kh, vt):
    nseq, _, seqlen, dh = qh.shape
    bs = MOBA_BLOCK
    assert seqlen % (2 * bs) == 0
    nblk = seqlen // bs
    nblk_pad = -(-nblk // 8) * 8
    nh = MOBA_STEP_HEADS
    kern = functools.partial(_moba_prompt_kernel, nblk=nblk, nblk_pad=nblk_pad)
    return pl.pallas_call(
        kern,
        grid=(nseq, MOBA_HEADS // nh, nblk),
        in_specs=[pl.BlockSpec((None, nh, bs, dh), lambda b, h, i: (b, h, i, 0)),
                  pl.BlockSpec((None, nh, seqlen, dh), lambda b, h, i: (b, h, 0, 0)),
                  pl.BlockSpec((None, nh, MOBA_VROWS, seqlen), lambda b, h, i: (b, h, 0, 0))],
        out_specs=pl.BlockSpec((bs, nh * dh), lambda b, h, i: (b * nblk + i, h)),
        out_shape=jax.ShapeDtypeStruct((nseq * seqlen, MOBA_WIDTH), F32),
        scratch_shapes=[pltpu.VMEM((nh, nblk_pad, dh), F32), pltpu.VMEM((nh, nblk_pad, bs), F32),
                        pltpu.VMEM((nh, 1, bs), F32),
                        pltpu.VMEM((nh, MOBA_VROWS, bs), F32), pltpu.VMEM((2, nh, 1, bs), F32),
                        pltpu.VMEM((2, nh, 1, bs), F32),
                        pltpu.VMEM((2, nh, 2 * bs, bs), F32), pltpu.VMEM((2, nh, 2 * bs, bs), BF16)],
        compiler_params=_cparams(("parallel", "parallel", "arbitrary")),
        name="moba_prompt",
    )(qh, kh, vt)


def _moba_gate_kernel(pt_ref, qbd_ref, *rest, ppg):
    del pt_ref
    k_refs = rest[:ppg]
    sel_ref, g_ref = rest[ppg:]
    s_id = pl.program_id(1)
    lane_blk = lax.broadcasted_iota(jnp.int32, (MOBA_HEADS, LANE), 1)

    @pl.when(s_id == 0)
    def _():
        g_ref[...] = jnp.full_like(g_ref, NEG_INF)

    qbd = qbd_ref[...].astype(BF16)
    g = g_ref[...]
    for bi in range(ppg // 2):
        s = jnp.zeros((MOBA_HEADS, PAGE_SIZE), F32)
        for pg in range(2):
            kt = k_refs[2 * bi + pg][...].reshape(MOBA_WIDTH, PAGE_SIZE)
            s = s + _mm(qbd, kt.astype(BF16))
        gate = jnp.sum(s, axis=1, keepdims=True)
        g = jnp.where(lane_blk == s_id * (ppg // 2) + bi, gate, g)
    g_ref[...] = g

    @pl.when(s_id == pl.num_programs(1) - 1)
    def _():
        gg = g_ref[...]
        lane_f = lane_blk.astype(F32)
        picked = jnp.zeros((MOBA_HEADS, LANE), F32)
        for r in range(MOBA_TOPK):
            mx = jnp.max(gg, axis=1, keepdims=True)
            idx = jnp.min(jnp.where(gg == mx, lane_f, float(LANE)), axis=1, keepdims=True)
            picked = jnp.where(lane_blk == r, idx, picked)
            gg = jnp.where(lane_f == idx, NEG_INF, gg)
        sel_ref[...] = picked.astype(jnp.int32)


PICK_HEADS = 8


def _moba_pick_kernel(pt_ref, sel_ref, q_ref, kn_ref, vn_ref, *rest):
    del pt_ref, sel_ref
    npg = 2 * MOBA_TOPK
    k_refs = rest[:PICK_HEADS * npg]
    v_refs = rest[PICK_HEADS * npg:2 * PICK_HEADS * npg]
    o_ref = rest[2 * PICK_HEADS * npg]
    scale = MOBA_HEAD_DIM ** -0.5
    sub = 8

    for hs in range(PICK_HEADS):
        q = q_ref[hs]
        q8 = jnp.broadcast_to((q * scale).astype(BF16), (sub, MOBA_HEAD_DIM))
        s_own = jnp.sum(q * kn_ref[hs], axis=1, keepdims=True) * scale
        s = [_mm(q8, k_refs[hs * npg + j][...].astype(BF16))[0:1, :] for j in range(npg)]
        m = s_own
        for sj in s:
            m = jnp.maximum(m, jnp.max(sj, axis=1, keepdims=True))
        p_own = jnp.exp(s_own - m)
        den = p_own
        num = p_own * vn_ref[hs]
        for j in range(npg):
            p = jnp.exp(s[j] - m)
            den = den + jnp.sum(p, axis=1, keepdims=True)
            p8 = jnp.broadcast_to(p.astype(BF16), (sub, PAGE_SIZE))
            num = num + _nt(p8, v_refs[hs * npg + j][...].astype(BF16))[0:1, :]
        o_ref[hs] = num / den


def _moba_decode(page_table, q, k_new, v_new, cache_kt, cache_vt, layer):
    nseq, n_pages = page_table.shape
    nblk = n_pages * PAGE_SIZE // MOBA_BLOCK
    assert n_pages * PAGE_SIZE == nblk * MOBA_BLOCK and MOBA_TOPK <= nblk <= LANE
    ppg = 16 if n_pages % 16 == 0 else 2
    heads, dh = MOBA_HEADS, MOBA_HEAD_DIM

    qbd = (q[:, :, None, :] * jnp.eye(heads, dtype=F32)[None, :, :, None]).reshape(nseq, heads, heads * dh)

    def all_heads(p):
        return pl.BlockSpec((None, None, heads, dh, PAGE_SIZE),
                            lambda b, s, pt, p=p: (pt[b, s * ppg + p], layer, 0, 0, 0))

    sel = pl.pallas_call(
        functools.partial(_moba_gate_kernel, ppg=ppg),
        grid_spec=pltpu.PrefetchScalarGridSpec(
            num_scalar_prefetch=1,
            grid=(nseq, n_pages // ppg),
            in_specs=[pl.BlockSpec((None, heads, heads * dh), lambda b, s, pt: (b, 0, 0))]
            + [all_heads(p) for p in range(ppg)],
            out_specs=pl.BlockSpec((None, heads, LANE), lambda b, s, pt: (b, 0, 0)),
            scratch_shapes=[pltpu.VMEM((heads, LANE), F32)],
        ),
        out_shape=jax.ShapeDtypeStruct((nseq, heads, LANE), jnp.int32),
        compiler_params=_cparams(("parallel", "arbitrary")),
        name="moba_gate",
    )(page_table, qbd, *([cache_kt] * ppg))
    sel_flat = sel[:, :, :MOBA_TOPK].reshape(nseq * heads * MOBA_TOPK)

    def one_head(hs, j):
        def index(b, g, pt, sl):
            h = g * PICK_HEADS + hs
            blk = sl[(b * heads + h) * MOBA_TOPK + j // 2]
            return (pt[b, 2 * blk + j % 2], layer, h, 0, 0)
        return pl.BlockSpec((None, None, None, dh, PAGE_SIZE), index)

    row = pl.BlockSpec((None, PICK_HEADS, 1, dh), lambda b, g, pt, sl: (b, g, 0, 0))
    rows = lambda a: a.reshape(nseq, heads, 1, dh)
    pages = [one_head(hs, j) for hs in range(PICK_HEADS) for j in range(2 * MOBA_TOPK)]
    out = pl.pallas_call(
        _moba_pick_kernel,
        grid_spec=pltpu.PrefetchScalarGridSpec(
            num_scalar_prefetch=2,
            grid=(nseq, heads // PICK_HEADS),
            in_specs=[row, row, row] + pages + pages,
            out_specs=row,
        ),
        out_shape=jax.ShapeDtypeStruct((nseq, heads, 1, dh), F32),
        compiler_params=_cparams(("parallel", "parallel")),
        name="moba_pick",
    )(page_table, sel_flat, rows(q), rows(k_new), rows(v_new),
      *([cache_kt] * len(pages)), *([cache_vt] * len(pages)))
    return out.reshape(nseq, heads * dh)


def _s5_disc_kernel(are_ref, aim_ref, ls_ref, bre_ref, bim_ref, abr_ref, abi_ref, bbr_ref, bbi_ref):
    a_re = are_ref[...]
    a_im = aim_ref[...]
    step = jnp.exp(ls_ref[...])
    mag = jnp.exp(a_re * step)
    ab_re = mag * jnp.cos(a_im * step)
    ab_im = mag * jnp.sin(a_im * step)
    den = a_re * a_re + a_im * a_im
    n_re = ab_re - 1.0
    f_re = (n_re * a_re + ab_im * a_im) / den
    f_im = (ab_im * a_re - n_re * a_im) / den
    b_re = bre_ref[...]
    b_im = bim_ref[...]
    abr_ref[...] = ab_re
    abi_ref[...] = ab_im
    bbr_ref[...] = f_re * b_re - f_im * b_im
    bbi_ref[...] = f_re * b_im + f_im * b_re


def _s5_discretise(a_re, a_im, log_step, b_re, b_im):
    depth = a_re.shape[0]
    rows = depth * S5_GROUPS * S5_GROUP
    rep = lambda a: jnp.broadcast_to(a[:, :, None, :], (depth, S5_GROUPS, S5_GROUP, S5_STATE)).reshape(rows, S5_STATE)
    ls = jnp.broadcast_to(log_step[:, :, None, None], (depth, S5_GROUPS, S5_GROUP, S5_STATE)).reshape(rows, S5_STATE)
    bt = lambda b: jnp.swapaxes(b, 2, 3).reshape(rows, S5_STATE)
    shp = jax.ShapeDtypeStruct((rows, S5_STATE), F32)
    abr, abi, bbr, bbi = pl.pallas_call(
        _s5_disc_kernel, out_shape=[shp] * 4, name="s5_disc",
    )(rep(a_re), rep(a_im), ls, bt(b_re), bt(b_im))
    r4 = lambda a: a.reshape(depth, S5_GROUPS, S5_GROUP, S5_STATE)
    ab = jnp.concatenate([r4(abr)[:, :, 0].reshape(depth, 1, S5_CH),
                          r4(abi)[:, :, 0].reshape(depth, 1, S5_CH)], axis=1)
    eye = jnp.eye(S5_GROUPS, dtype=F32)
    bd = lambda a: (r4(a)[:, :, :, None, :] * eye[None, :, None, :, None]).reshape(depth, S5_WIDTH, S5_CH)
    bb = jnp.concatenate([bd(bbr), bd(bbi)], axis=2)
    return ab, bb.astype(BF16)


def _s5_out_map(c_re, c_im):
    depth = c_re.shape[0]
    eye = jnp.eye(S5_GROUPS, dtype=F32)
    bd = lambda c: (jnp.swapaxes(c, 2, 3)[:, :, :, None, :] * eye[None, :, None, :, None]).reshape(depth, S5_CH, S5_WIDTH)
    return jnp.concatenate([bd(c_re), -bd(c_im)], axis=1).astype(BF16)


S5_SCAN_LANES = 512


def _s5_kernel(u_ref, bb_ref, ab_ref, h0_ref, cc_ref, d_ref, wglu_ref, bglu_ref,
               o_ref, hfin_ref, xs_ref, hc_ref, *, nb, tt):
    @pl.when(pl.program_id(0) == 0)
    def _():
        hc_ref[...] = h0_ref[...]

    u = u_ref[...]
    xs_ref[...] = _mm(u.astype(BF16), bb_ref[...])

    for c0 in range(0, S5_CH, S5_SCAN_LANES):
        re_sl = slice(c0, c0 + S5_SCAN_LANES)
        im_sl = slice(S5_CH + c0, S5_CH + c0 + S5_SCAN_LANES)
        ar = jnp.broadcast_to(ab_ref[0:1, re_sl], (nb, S5_SCAN_LANES))
        ai = jnp.broadcast_to(ab_ref[1:2, re_sl], (nb, S5_SCAN_LANES))

        def step(t, carry, re_sl=re_sl, im_sl=im_sl, ar=ar, ai=ai):
            hr, hi = carry
            rows = pl.ds(pl.multiple_of(t * nb, nb), nb)
            nhr = ar * hr - ai * hi + xs_ref[rows, re_sl]
            nhi = ar * hi + ai * hr + xs_ref[rows, im_sl]
            xs_ref[rows, re_sl] = nhr
            xs_ref[rows, im_sl] = nhi
            return nhr, nhi

        hr, hi = lax.fori_loop(0, tt, step, (hc_ref[:, re_sl], hc_ref[:, im_sl]), unroll=min(4, tt))
        hc_ref[:, re_sl] = hr
        hc_ref[:, im_sl] = hi

    y = _mm(xs_ref[...].astype(BF16), cc_ref[...]) + d_ref[...] * u
    yg = _gelu_tanh(y)
    gate = _sigmoid(_mm(yg.astype(BF16), wglu_ref[...]) + bglu_ref[...])
    o_ref[...] = yg * gate
    hfin_ref[...] = hc_ref[...]


def _s5(u_tm, nb, bb, ab, h0, cc, d, wglu, bglu):
    rows = u_tm.shape[0]
    t_total = rows // nb
    tt = min(128, t_total)
    const = lambda shape: pl.BlockSpec(shape, lambda t: (0,) * len(shape))
    kern = functools.partial(_s5_kernel, nb=nb, tt=tt)
    return pl.pallas_call(
        kern,
        grid=(t_total // tt,),
        in_specs=[pl.BlockSpec((tt * nb, S5_WIDTH), lambda t: (t, 0)),
                  const((S5_WIDTH, 2 * S5_CH)), const((2, S5_CH)), const((nb, 2 * S5_CH)),
                  const((2 * S5_CH, S5_WIDTH)), const((1, S5_WIDTH)),
                  const((S5_WIDTH, S5_WIDTH)), const((1, S5_WIDTH))],
        out_specs=[pl.BlockSpec((tt * nb, S5_WIDTH), lambda t: (t, 0)), const((nb, 2 * S5_CH))],
        out_shape=[jax.ShapeDtypeStruct((rows, S5_WIDTH), F32), jax.ShapeDtypeStruct((nb, 2 * S5_CH), F32)],
        scratch_shapes=[pltpu.VMEM((tt * nb, 2 * S5_CH), F32), pltpu.VMEM((nb, 2 * S5_CH), F32)],
        compiler_params=_cparams(("arbitrary",)),
        name="s5_scan",
    )(u_tm, bb, ab, h0, cc, d, wglu, bglu)


def _split3(x):
    hi = x.astype(BF16)
    r = x - hi.astype(F32)
    mid = r.astype(BF16)
    lo = (r - mid.astype(F32)).astype(BF16)
    return hi, mid, lo


def _gla_prompt_kernel(gq_ref, gk_ref, gv_ref, gr_ref, lf_ref, gn_ref, o_ref, sfin_ref, st_ref, *, nchunk):
    c = GLA_CHUNK

    @pl.when(pl.program_id(1) == 0)
    def _():
        st_ref[...] = jnp.zeros_like(st_ref)

    row = lax.broadcasted_iota(jnp.int32, (c, c), 0)
    col = lax.broadcasted_iota(jnp.int32, (c, c), 1)
    causal = col <= row
    tri = causal.astype(BF16)
    gn = gn_ref[...]

    for ci in range(nchunk):
        rows = slice(ci * c, (ci + 1) * c)
        g_hi, g_mid, g_lo = _split3(lf_ref[rows, :])
        bcum = _mm(tri, g_hi) + _mm(tri, g_mid) + _mm(tri, g_lo)
        blast = bcum[c - 1:c, :]
        qd = (gq_ref[rows, :] * jnp.exp(bcum)).astype(BF16)
        kd = (gk_ref[rows, :] * jnp.exp(-bcum)).astype(BF16)
        kt = (gk_ref[rows, :] * jnp.exp(blast - bcum)).astype(BF16)
        st = st_ref[...]
        st_ref[...] = st * jnp.exp(blast)
        st16 = st.astype(BF16)
        for hd in range(GLA_HEADS):
            ks = slice(hd * GLA_DK, (hd + 1) * GLA_DK)
            vs = slice(hd * GLA_DV, (hd + 1) * GLA_DV)
            vh = gv_ref[rows, vs].astype(BF16)
            att = jnp.where(causal, _nt(qd[:, ks], kd[:, ks]), 0.0)
            o = _nt(qd[:, ks], st16[:, ks]) + _mm(att.astype(BF16), vh)
            st_ref[:, ks] += _tn(vh, kt[:, ks])
            o_ref[rows, vs] = _rms(o, gn) * _silu(gr_ref[rows, vs])

    sfin_ref[...] = st_ref[...]


def _gla_prompt(gq, gk, gv, gr, lf, gnorm, nseq, seqlen):
    assert seqlen % GLA_CHUNK == 0
    tt = min(512, seqlen)
    nt = seqlen // tt
    tok = lambda width: pl.BlockSpec((tt, width), lambda b, t: (b * nt + t, 0))
    kern = functools.partial(_gla_prompt_kernel, nchunk=tt // GLA_CHUNK)
    return pl.pallas_call(
        kern,
        grid=(nseq, nt),
        in_specs=[tok(GLA_KEY_WIDTH), tok(GLA_KEY_WIDTH), tok(GLA_VAL_WIDTH), tok(GLA_VAL_WIDTH),
                  tok(GLA_KEY_WIDTH), pl.BlockSpec((1, GLA_DV), lambda b, t: (0, 0))],
        out_specs=[tok(GLA_VAL_WIDTH),
                   pl.BlockSpec((None, GLA_DV, GLA_KEY_WIDTH), lambda b, t: (b, 0, 0))],
        out_shape=[jax.ShapeDtypeStruct((nseq * seqlen, GLA_VAL_WIDTH), F32),
                   jax.ShapeDtypeStruct((nseq, GLA_DV, GLA_KEY_WIDTH), F32)],
        scratch_shapes=[pltpu.VMEM((GLA_DV, GLA_KEY_WIDTH), F32)],
        compiler_params=_cparams(("parallel", "arbitrary")),
        name="gla_prompt",
    )(gq, gk, gv, gr, lf, gnorm)


def _gla_decode_kernel(q_ref, k_ref, g_ref, v_ref, r_ref, s0_ref, gn_ref, o_ref, s_ref):
    s = jnp.exp(g_ref[...]) * s0_ref[...] + k_ref[...] * v_ref[...]
    s_ref[...] = s
    o = jnp.sum(q_ref[...] * s, axis=2, keepdims=True)
    o_ref[...] = _rms(o, gn_ref[...]) * _silu(r_ref[...])


def _gla_decode(gq, gk, lf, gv, gr, s0, gnorm):
    n = gq.shape[0]
    col = lambda a: a.reshape(n, GLA_HEADS, GLA_DK, 1)
    rowv = lambda a: a.reshape(n, GLA_HEADS, 1, GLA_DV)
    g = 8 if n % 8 == 0 else 1
    cspec = pl.BlockSpec((g, GLA_HEADS, GLA_DK, 1), lambda b: (b, 0, 0, 0))
    rspec = pl.BlockSpec((g, GLA_HEADS, 1, GLA_DV), lambda b: (b, 0, 0, 0))
    sspec = pl.BlockSpec((g, GLA_HEADS, GLA_DK, GLA_DV), lambda b: (b, 0, 0, 0))
    o, s = pl.pallas_call(
        _gla_decode_kernel,
        grid=(n // g,),
        in_specs=[cspec, cspec, cspec, rspec, rspec, sspec,
                  pl.BlockSpec((1, 1, 1, GLA_DV), lambda b: (0, 0, 0, 0))],
        out_specs=[rspec, sspec],
        out_shape=[jax.ShapeDtypeStruct((n, GLA_HEADS, 1, GLA_DV), F32),
                   jax.ShapeDtypeStruct((n, GLA_HEADS, GLA_DK, GLA_DV), F32)],
        compiler_params=_cparams(("parallel",)),
        name="gla_decode",
    )(col(gq), col(gk), col(lf), rowv(gv), rowv(gr), s0, gnorm.reshape(1, 1, 1, GLA_DV))
    return o.reshape(n, GLA_VAL_WIDTH), s


def _out_ffn_kernel(x_ref, a_ref, b_ref, c_ref, wo_ref, nf_ref, wg_ref, wu_ref, wd_ref, nfin_ref,
                    o_ref, hf_ref, acc_ref, *, final_norm):
    j = pl.program_id(2)

    @pl.when(j == 0)
    def _():
        o_b = MOBA_WIDTH + S5_WIDTH
        x1 = (x_ref[...] + _mm(a_ref[...].astype(BF16), wo_ref[:MOBA_WIDTH, :])
              + _mm(b_ref[...].astype(BF16), wo_ref[MOBA_WIDTH:o_b, :])
              + _mm(c_ref[...].astype(BF16), wo_ref[o_b:, :]))
        acc_ref[...] = x1
        hf_ref[...] = _rms(x1, nf_ref[...]).astype(BF16)

    hf = hf_ref[...]
    act = _silu(_mm(hf, wg_ref[...])) * _mm(hf, wu_ref[...])
    acc_ref[...] += _mm(act.astype(BF16), wd_ref[...])

    @pl.when(j == pl.num_programs(2) - 1)
    def _():
        y = acc_ref[...]
        o_ref[...] = _rms(y, nfin_ref[...]) if final_norm else y


def _out_ffn(x2d, a, b_tm, c, nseq, seqlen, w_out, norm_ffn, w_gate, w_up, w_down, norm_final, final_norm):
    tm = min(512, seqlen)
    nt = seqlen // tm
    th = FFN_HIDDEN // 2
    nh = FFN_HIDDEN // th
    tok = lambda width: pl.BlockSpec((tm, width), lambda b, t, j: (b * nt + t, 0))
    const = lambda shape: pl.BlockSpec(shape, lambda b, t, j: (0,) * len(shape))
    kern = functools.partial(_out_ffn_kernel, final_norm=final_norm)
    return pl.pallas_call(
        kern,
        grid=(nseq, nt, nh),
        in_specs=[tok(D_MODEL), tok(MOBA_WIDTH),
                  pl.BlockSpec((tm, S5_WIDTH), lambda b, t, j: (t, b)),
                  tok(GLA_VAL_WIDTH), const((D_MODEL, D_MODEL)), const((1, D_MODEL)),
                  pl.BlockSpec((D_MODEL, th), lambda b, t, j: (0, j)),
                  pl.BlockSpec((D_MODEL, th), lambda b, t, j: (0, j)),
                  pl.BlockSpec((th, D_MODEL), lambda b, t, j: (j, 0)),
                  const((1, D_MODEL))],
        out_specs=tok(D_MODEL),
        out_shape=jax.ShapeDtypeStruct((nseq * seqlen, D_MODEL), F32),
        scratch_shapes=[pltpu.VMEM((tm, D_MODEL), BF16), pltpu.VMEM((tm, D_MODEL), F32)],
        compiler_params=_cparams(("parallel", "parallel", "arbitrary")),
        name="out_ffn",
    )(x2d, a, b_tm, c, w_out, norm_ffn, w_gate, w_up, w_down, norm_final)


def _prep_weights(w_in, w_out, s5_w_glu, gla_w_gate, w_ffn_gate, w_ffn_up, w_ffn_down):
    depth = w_in.shape[0]
    w_in_p = jnp.pad(w_in, ((0, 0), (0, 0), (0, IN_COLS_PAD - IN_COLS))).astype(BF16)
    wg_p = jnp.pad(gla_w_gate, ((0, 0), (0, LANE - GLA_GATE_RANK), (0, 0))).astype(BF16)
    del depth
    return (w_in_p, w_out.astype(BF16), s5_w_glu.astype(BF16), wg_p,
            w_ffn_gate.astype(BF16), w_ffn_up.astype(BF16), w_ffn_down.astype(BF16))


def _trunk(x, kv_pool, ssm_re0, ssm_im0, gla0, params):
    (norm_mix, w_in_p, w_out, ab, bb, cc, s5_d, w_glu, s5_b_glu, wg_p, gla_b_gate, gla_norm,
     norm_ffn, w_gate, w_up, w_down, norm_final) = params
    nseq, seqlen, _ = x.shape
    depth = w_in_p.shape[0]
    decode = kv_pool is not None
    if decode:
        assert seqlen == 1
        cache_k, cache_v, page_table = kv_pool
        cache_kt = jnp.transpose(cache_k, (0, 1, 3, 4, 2))
        cache_vt = jnp.transpose(cache_v, (0, 1, 3, 4, 2))
        tw_nseq, tw_len = 1, nseq
    else:
        assert seqlen % MOBA_BLOCK == 0
        tw_nseq, tw_len = nseq, seqlen
    n = nseq * seqlen
    x2d = x.reshape(n, D_MODEL)
    k_rows, v_rows, s_re, s_im, s_gla = [], [], [], [], []
    for l in range(depth):
        k, v, *q_parts, u_tm, gq, gk, gv, gr, lf = _in_proj(
            x2d, tw_nseq, tw_len, norm_mix[l][None], w_in_p[l], wg_p[l], gla_b_gate[l][None],
            head_major=not decode, layer=l, depth=depth, kv_stack=(k_rows[-1], v_rows[-1]) if l else None)
        k_rows.append(k)
        v_rows.append(v)
        if decode:
            heads = lambda a: a.reshape(nseq, MOBA_HEADS, MOBA_HEAD_DIM)
            a_out = _moba_decode(page_table, heads(q_parts[0]), heads(k), heads(v), cache_kt, cache_vt, l)
        else:
            a_out = _moba_prompt(*q_parts)
        if ssm_re0 is None:
            h0 = jnp.zeros((nseq, 2 * S5_CH), F32)
        else:
            h0 = jnp.concatenate([ssm_re0[l].reshape(nseq, S5_CH), ssm_im0[l].reshape(nseq, S5_CH)], axis=1)
        b_tm, h_fin = _s5(u_tm.reshape(n, S5_WIDTH), nseq, bb[l], ab[l], h0, cc[l], s5_d[l][None],
                          w_glu[l], s5_b_glu[l][None])
        s_re.append(h_fin[:, :S5_CH].reshape(nseq, S5_GROUPS, S5_STATE))
        s_im.append(h_fin[:, S5_CH:].reshape(nseq, S5_GROUPS, S5_STATE))
        b_tm = b_tm.reshape(tw_len, tw_nseq * S5_WIDTH)
        if decode:
            c_out, s_fin = _gla_decode(gq, gk, lf, gv, gr, gla0[l], gla_norm[l])
        else:
            c_out, s_t = _gla_prompt(gq, gk, gv, gr, lf, gla_norm[l][None], nseq, seqlen)
            s_fin = jnp.transpose(s_t.reshape(nseq, GLA_DV, GLA_HEADS, GLA_DK), (0, 2, 3, 1))
        s_gla.append(s_fin)
        x2d = _out_ffn(x2d, a_out, b_tm, c_out, tw_nseq, tw_len, w_out[l], norm_ffn[l][None],
                       w_gate[l], w_up[l], w_down[l], norm_final[None], final_norm=(l == depth - 1))
    if decode:
        shp = (nseq, seqlen, MOBA_HEADS, MOBA_HEAD_DIM)
        stack_kv = lambda rows: jnp.stack([r.reshape(shp) for r in rows], axis=1)
    else:
        shp = (nseq, depth, MOBA_HEADS, MOBA_HEAD_DIM, seqlen)
        stack_kv = lambda rows: jnp.transpose(rows[-1].reshape(shp), (0, 1, 4, 2, 3))
    return (x2d.reshape(nseq, seqlen, D_MODEL), stack_kv(k_rows), stack_kv(v_rows),
            jnp.stack(s_re, axis=0), jnp.stack(s_im, axis=0), jnp.stack(s_gla, axis=0))


def kernel(x_prompt, x_sample, cache_k, cache_v, page_table, state_ssm_re, state_ssm_im, state_gla,
           norm_mix, w_in, w_out, s5_a_re, s5_a_im, s5_log_step, s5_b_re, s5_b_im, s5_c_re, s5_c_im,
           s5_d, s5_w_glu, s5_b_glu, gla_w_gate, gla_b_gate, gla_norm, norm_ffn, w_ffn_gate, w_ffn_up,
           w_ffn_down, norm_final):
    w_in_p, w_out16, w_glu16, wg_p, w_gate16, w_up16, w_down16 = _prep_weights(
        w_in, w_out, s5_w_glu, gla_w_gate, w_ffn_gate, w_ffn_up, w_ffn_down)
    ab, bb = _s5_discretise(s5_a_re, s5_a_im, s5_log_step, s5_b_re, s5_b_im)
    cc = _s5_out_map(s5_c_re, s5_c_im)
    params = (norm_mix, w_in_p, w_out16, ab, bb, cc, s5_d, w_glu16, s5_b_glu, wg_p, gla_b_gate, gla_norm,
              norm_ffn, w_gate16, w_up16, w_down16, norm_final)
    y_p, k_p, v_p, re_p, im_p, gla_p = _trunk(x_prompt, None, None, None, None, params)
    y_s, k_s, v_s, re_s, im_s, gla_s = _trunk(x_sample, (cache_k, cache_v, page_table),
                                              state_ssm_re, state_ssm_im, state_gla, params)
    return (y_p, y_s, k_p, v_p, k_s, v_s, re_p, im_p, re_s, im_s, gla_p, gla_s)
```
